```python
import math
import jax, jax.numpy as jnp
from jax import lax
import numpy as np

D_MODEL = 2048
BATCH = 1
SEQ = 8192
DEPTH = 2

N_META = 16
F_WIDTH = D_MODEL // 2
F_GROUPS = 4
F_GROUP = F_WIDTH // F_GROUPS
N_HEADS = 16
QK_NOPE = 128
QK_ROPE = 64
QK_HEAD = QK_NOPE + QK_ROPE
V_HEAD = 128
Q_LORA = 768
KV_LORA = 512
ROPE_THETA = 10000.0
C_WIDTH = D_MODEL // 2
N_BRANCH = 3
D_FF = 5632
Q_BLOCK = 128
EPS = 1e-6

OFF_Q = F_WIDTH
OFF_KV = OFF_Q + Q_LORA
OFF_KR = OFF_KV + KV_LORA
OFF_C = OFF_KR + QK_ROPE
OFF_G = OFF_C + 3 * C_WIDTH
N_IN = OFF_G + N_BRANCH * D_MODEL

kernel_name = "hybrid_fnet_mla_shortconv_convffn_encoder"


def _rms(x, g):
    xf = x.astype(jnp.float32)
    y = xf * lax.rsqrt(jnp.mean(xf * xf, axis=-1, keepdims=True) + EPS)
    return (y * g.astype(jnp.float32)).astype(x.dtype)


def _dwconv3(h, w):
    hp = jnp.pad(h, ((0, 0), (1, 1), (0, 0)))
    return hp[:, :-2] * w[0] + hp[:, 1:-1] * w[1] + hp[:, 2:] * w[2]


def _rope_tables(L):
    inv = 1.0 / (ROPE_THETA ** (jnp.arange(0, QK_ROPE, 2, dtype=jnp.float32) / QK_ROPE))
    ang = jnp.arange(L, dtype=jnp.float32)[:, None] * inv[None, :]
    return jnp.cos(ang)[None, :, None, :], jnp.sin(ang)[None, :, None, :]


def _apply_rope(x, cos, sin):
    xf = x.astype(jnp.float32)
    half = QK_ROPE // 2
    x1, x2 = xf[..., :half], xf[..., half:]
    out = jnp.concatenate([x1 * cos - x2 * sin, x2 * cos + x1 * sin], axis=-1)
    return out.astype(x.dtype)


def _attend(q, k, v):
    s = jnp.einsum('bqhd,bkhd->bhqk', q, k).astype(jnp.float32) * (1.0 / math.sqrt(QK_HEAD))
    p = jax.nn.softmax(s, axis=-1)
    return jnp.einsum('bhqk,bkhd->bqhd', p.astype(v.dtype), v)


def _dense_attention(q, k, v):
    B = q.shape[0]
    out_meta = _attend(q[:, :N_META], k, v)
    q_real = q[:, N_META:]
    nb = q_real.shape[1] // Q_BLOCK
    qb = q_real.reshape(B, nb, Q_BLOCK, N_HEADS, QK_HEAD).transpose(1, 0, 2, 3, 4)
    ob = lax.map(lambda qq: _attend(qq, k, v), qb)
    out_real = ob.transpose(1, 0, 2, 3, 4).reshape(B, nb * Q_BLOCK, N_HEADS, V_HEAD)
    return jnp.concatenate([out_meta, out_real], axis=1)


def _mixer(xn, cos, sin, w_in, g_qa, g_kva, w_uq, w_ukv, g_q, g_k, conv_c,
           w_pa, w_pb, w_pc, w_o):
    B, L, _ = xn.shape
    p = xn @ w_in
    a = p[..., :OFF_Q]
    cq = p[..., OFF_Q:OFF_KV]
    ckv = p[..., OFF_KV:OFF_KR]
    k_rope = p[..., OFF_KR:OFF_C]
    cb, cc, ch = jnp.split(p[..., OFF_C:OFF_G], 3, axis=-1)
    gates = p[..., OFF_G:]

    fa = jnp.fft.fft2(a.reshape(B, L, F_GROUPS, F_GROUP).astype(jnp.float32),
                      axes=(1, 3), norm='ortho').real
    ya = fa.reshape(B, L, F_WIDTH).astype(xn.dtype) @ w_pa

    q = (_rms(cq, g_qa) @ w_uq).reshape(B, L, N_HEADS, QK_HEAD)
    kv = (_rms(ckv, g_kva) @ w_ukv).reshape(B, L, N_HEADS, QK_NOPE + V_HEAD)
    k_nope, v = kv[..., :QK_NOPE], kv[..., QK_NOPE:]
    k_r = jnp.broadcast_to(k_rope[:, :, None, :], (B, L, N_HEADS, QK_ROPE))
    k = jnp.concatenate([k_nope, k_r], axis=-1)
    q = _rms(q, g_q)
    k = _rms(k, g_k)
    q = jnp.concatenate([q[..., :QK_NOPE], _apply_rope(q[..., QK_NOPE:], cos, sin)], axis=-1)
    k = jnp.concatenate([k[..., :QK_NOPE], _apply_rope(k[..., QK_NOPE:], cos, sin)], axis=-1)
    o = _dense_attention(q, k, v)
    yb = o.reshape(B, L, N_HEADS * V_HEAD) @ w_pb

    yc = (cb * _dwconv3(cc * ch, conv_c)) @ w_pc

    g = jax.nn.sigmoid(gates.astype(jnp.float32)).astype(xn.dtype).reshape(B, L, N_BRANCH, D_MODEL)
    merged = g[..., 0, :] * ya + g[..., 1, :] * yb + g[..., 2, :] * yc
    return merged @ w_o


def _conv_ffn(xn, w_up, conv_ffn, w_down):
    h = _dwconv3(xn @ w_up, conv_ffn)
    a, b = h[..., :D_FF], h[..., D_FF:]
    return (jax.nn.silu(a) * b) @ w_down


def setup_inputs(seed: int = 0) -> dict:
    key = jax.random.key(seed)
    ks = jax.random.split(key, 20)
    f32 = jnp.float32

    def nrm(k, shape, scale):
        return jax.random.normal(k, shape, f32) * scale

    def gain(k, shape):
        return 1.0 + 0.01 * jax.random.normal(k, shape, f32)

    return {
        "x": nrm(ks[0], (BATCH, SEQ, D_MODEL), 1.0),
        "meta_tokens": nrm(ks[1], (N_META, D_MODEL), 1.0),
        "g_mix": gain(ks[2], (DEPTH, D_MODEL)),
        "w_in": nrm(ks[3], (DEPTH, D_MODEL, N_IN), D_MODEL ** -0.5),
        "g_qa": gain(ks[4], (DEPTH, Q_LORA)),
        "g_kva": gain(ks[5], (DEPTH, KV_LORA)),
        "w_uq": nrm(ks[6], (DEPTH, Q_LORA, N_HEADS * QK_HEAD), Q_LORA ** -0.5),
        "w_ukv": nrm(ks[7], (DEPTH, KV_LORA, N_HEADS * (QK_NOPE + V_HEAD)), KV_LORA ** -0.5),
        "g_q": gain(ks[8], (DEPTH, QK_HEAD)),
        "g_k": gain(ks[9], (DEPTH, QK_HEAD)),
        "conv_c": nrm(ks[10], (DEPTH, 3, C_WIDTH), 3 ** -0.5),
        "w_pa": nrm(ks[11], (DEPTH, F_WIDTH, D_MODEL), F_WIDTH ** -0.5),
        "w_pb": nrm(ks[12], (DEPTH, N_HEADS * V_HEAD, D_MODEL), (N_HEADS * V_HEAD) ** -0.5),
        "w_pc": nrm(ks[13], (DEPTH, C_WIDTH, D_MODEL), C_WIDTH ** -0.5),
        "w_o": nrm(ks[14], (DEPTH, D_MODEL, D_MODEL), D_MODEL ** -0.5),
        "g_ffn": gain(ks[15], (DEPTH, D_MODEL)),
        "w_up": nrm(ks[16], (DEPTH, D_MODEL, 2 * D_FF), D_MODEL ** -0.5),
        "conv_ffn": nrm(ks[17], (DEPTH, 3, 2 * D_FF), 3 ** -0.5),
        "w_down": nrm(ks[18], (DEPTH, D_FF, D_MODEL), D_FF ** -0.5),
    }


def reference(x, meta_tokens, g_mix, w_in, g_qa, g_kva, w_uq, w_ukv, g_q, g_k, conv_c,
              w_pa, w_pb, w_pc, w_o, g_ffn, w_up, conv_ffn, w_down):
    B = x.shape[0]
    meta = jnp.broadcast_to(meta_tokens[None].astype(x.dtype), (B, N_META, D_MODEL))
    h = jnp.concatenate([meta, x], axis=1)
    cos, sin = _rope_tables(h.shape[1])
    for l in range(DEPTH):
        h = h + _mixer(_rms(h, g_mix[l]), cos, sin, w_in[l], g_qa[l], g_kva[l],
                       w_uq[l], w_ukv[l], g_q[l], g_k[l], conv_c[l],
                       w_pa[l], w_pb[l], w_pc[l], w_o[l])
        h = h + _conv_ffn(_rms(h, g_ffn[l]), w_up[l], conv_ffn[l], w_down[l])
    return h[:, N_META:]
```

```python
import functools
import math

import jax
import jax.numpy as jnp
import numpy as np
from jax import lax
from jax.experimental import pallas as pl
from jax.experimental.pallas import tpu as pltpu

D_MODEL = 2048
N_META = 16
F_WIDTH = D_MODEL // 2
F_GROUPS = 4
F_GROUP = F_WIDTH // F_GROUPS
N_HEADS = 16
QK_NOPE = 128
QK_ROPE = 64
QK_HEAD = QK_NOPE + QK_ROPE
V_HEAD = 128
Q_LORA = 768
KV_LORA = 512
ROPE_THETA = 10000.0
C_WIDTH = D_MODEL // 2
N_BRANCH = 3
D_FF = 5632
EPS = 1e-6

OFF_Q = F_WIDTH
OFF_KV = OFF_Q + Q_LORA
OFF_KR = OFF_KV + KV_LORA
OFF_C = OFF_KR + QK_ROPE
OFF_G = OFF_C + 3 * C_WIDTH

LANE = 128
MXU_DIM = 256
P_G = 0
P_A = P_G + N_BRANCH * D_MODEL
P_CB = P_A + F_WIDTH
P_CC = P_CB + C_WIDTH
P_CH = P_CC + C_WIDTH
P_KV = P_CH + C_WIDTH
P_Q = P_KV + KV_LORA
P_KR = P_Q + Q_LORA
P_N = P_KR + 2 * QK_ROPE

VMEM_LIMIT_BYTES = 56 * 1024 * 1024

BF16 = jnp.bfloat16
F32 = jnp.float32


def _params(*semantics):
    return pltpu.CompilerParams(dimension_semantics=semantics, vmem_limit_bytes=VMEM_LIMIT_BYTES)


def _row_tile(n, target, mult=16):
    best = None
    for d in range(mult, min(n, target) + 1, mult):
        if n % d == 0:
            best = d
    assert best is not None, (n, target, mult)
    return best


def _rms_matmul_kernel(x_ref, g_ref, w_ref, o_ref, xn_ref):
    @pl.when(pl.program_id(1) == 0)
    def _():
        x = x_ref[...].astype(F32)
        ms = jnp.mean(x * x, axis=-1, keepdims=True)
        xn_ref[...] = (x * lax.rsqrt(ms + EPS) * g_ref[...]).astype(xn_ref.dtype)

    o_ref[...] = jnp.dot(xn_ref[...], w_ref[...], preferred_element_type=F32).astype(o_ref.dtype)


def _rms_matmul(x, g, w, *, tm, tn, out_dtype=BF16):
    m, k = x.shape
    n = w.shape[1]
    return pl.pallas_call(
        _rms_matmul_kernel,
        out_shape=jax.ShapeDtypeStruct((m, n), out_dtype),
        grid=(m // tm, n // tn),
        in_specs=[
            pl.BlockSpec((tm, k), lambda i, j: (i, 0)),
            pl.BlockSpec((1, k), lambda i, j: (0, 0)),
            pl.BlockSpec((k, tn), lambda i, j: (0, j)),
        ],
        out_specs=pl.BlockSpec((tm, tn), lambda i, j: (i, j)),
        scratch_shapes=[pltpu.VMEM((tm, k), BF16)],
        compiler_params=_params("parallel", "arbitrary"),
        name="rms_matmul",
    )(x, g, w)


def _chan_dft_kernel(a_ref, w_ref, o_ref, *, n_valid_tiles):
    @pl.when(pl.program_id(0) < n_valid_tiles)
    def _():
        y = jnp.dot(a_ref[...], w_ref[...], preferred_element_type=F32)
        o_ref[0] = y[:, :F_GROUP].astype(o_ref.dtype)
        o_ref[1] = y[:, F_GROUP:].astype(o_ref.dtype)

    @pl.when(pl.program_id(0) >= n_valid_tiles)
    def _():
        o_ref[...] = jnp.zeros_like(o_ref)


def _chan_dft(p, w_cs, *, tm, m_pad):
    m = p.shape[0]
    base = P_A // F_GROUP
    nt = m // tm
    return pl.pallas_call(
        functools.partial(_chan_dft_kernel, n_valid_tiles=nt),
        out_shape=jax.ShapeDtypeStruct((2, m_pad, F_WIDTH), BF16),
        grid=(pl.cdiv(m_pad, tm), F_GROUPS),
        in_specs=[
            pl.BlockSpec((tm, F_GROUP), lambda i, g: (jnp.minimum(i, nt - 1), base + g)),
            pl.BlockSpec((F_GROUP, 2 * F_GROUP), lambda i, g: (0, 0)),
        ],
        out_specs=pl.BlockSpec((2, tm, F_GROUP), lambda i, g: (0, i, g)),
        compiler_params=_params("parallel", "arbitrary"),
        name="chan_dft",
    )(p, w_cs)


def _matmul_acc_kernel(a_ref, b_ref, o_ref, acc_ref):
    kk = pl.program_id(1)

    @pl.when(kk == 0)
    def _():
        acc_ref[...] = jnp.zeros_like(acc_ref)

    acc_ref[...] += jnp.dot(a_ref[...], b_ref[...], preferred_element_type=F32)

    @pl.when(kk == pl.num_programs(1) - 1)
    def _():
        o_ref[...] = acc_ref[...].astype(o_ref.dtype)


def _matmul_acc(a, b, *, tm, tk, out_dtype=BF16):
    m, k = a.shape
    n = b.shape[1]
    return pl.pallas_call(
        _matmul_acc_kernel,
        out_shape=jax.ShapeDtypeStruct((m, n), out_dtype),
        grid=(m // tm, k // tk),
        in_specs=[
            pl.BlockSpec((tm, tk), lambda i, kk: (i, kk)),
            pl.BlockSpec((tk, n), lambda i, kk: (kk, 0)),
        ],
        out_specs=pl.BlockSpec((tm, n), lambda i, kk: (i, 0)),
        scratch_shapes=[pltpu.VMEM((tm, n), F32)],
        compiler_params=_params("parallel", "arbitrary"),
        name="pos_dft",
    )(a, b)


def _rope_mix(blk, gain_ref, cs_ref):
    t = blk * gain_ref[...] * cs_ref[...]
    return t + pltpu.roll(t, QK_ROPE, 1)


def _first_half_sumsq(blk):
    lane = lax.broadcasted_iota(jnp.int32, blk.shape, 1)
    return jnp.sum(jnp.where(lane < QK_ROPE, blk * blk, 0.0), axis=-1, keepdims=True)


def _q_prep_kernel(cq_ref, gqa_ref, w_ref, gn_ref, gr_ref, cs_ref, o_ref, xn_ref):
    @pl.when(pl.program_id(1) == 0)
    def _():
        x = cq_ref[...].astype(F32)
        ms = jnp.mean(x * x, axis=-1, keepdims=True)
        xn_ref[...] = (x * lax.rsqrt(ms + EPS) * gqa_ref[...]).astype(xn_ref.dtype)

    y = jnp.dot(xn_ref[...], w_ref[...], preferred_element_type=F32)
    y0 = y[:, :QK_NOPE]
    y1 = y[:, QK_NOPE:]
    ss = jnp.sum(y0 * y0, axis=-1, keepdims=True) + _first_half_sumsq(y1)
    scale = lax.rsqrt(ss * (1.0 / QK_HEAD) + EPS) * (1.0 / math.sqrt(QK_HEAD))
    r = _rope_mix(y1, gr_ref, cs_ref)
    o_ref[:, :QK_NOPE] = (y0 * gn_ref[...] * scale).astype(o_ref.dtype)
    o_ref[:, QK_NOPE:] = (r[:, :QK_ROPE] * scale).astype(o_ref.dtype)


def _q_prep(p, g_qa, w_q, gn, gr, cs, *, tm):
    m = p.shape[0]
    return pl.pallas_call(
        _q_prep_kernel,
        out_shape=jax.ShapeDtypeStruct((N_HEADS, m, QK_HEAD), BF16),
        grid=(m // tm, N_HEADS),
        in_specs=[
            pl.BlockSpec((tm, Q_LORA), lambda i, h: (i, P_Q // Q_LORA)),
            pl.BlockSpec((1, Q_LORA), lambda i, h: (0, 0)),
            pl.BlockSpec((None, Q_LORA, 2 * LANE), lambda i, h: (h, 0, 0)),
            pl.BlockSpec((1, QK_NOPE), lambda i, h: (0, 0)),
            pl.BlockSpec((1, 2 * QK_ROPE), lambda i, h: (0, 0)),
            pl.BlockSpec((tm, 2 * QK_ROPE), lambda i, h: (i, 0)),
        ],
        out_specs=pl.BlockSpec((None, tm, QK_HEAD), lambda i, h: (h, i, 0)),
        scratch_shapes=[pltpu.VMEM((tm, Q_LORA), BF16)],
        compiler_params=_params("parallel", "arbitrary"),
        name="q_prep",
    )(p, g_qa, w_q, gn, gr, cs)


def _kv_prep_kernel(ckv_ref, kr_ref, gkva_ref, w_ref, gn_ref, gr_ref, cs_ref,
                    k_ref, v_ref, xn_ref, rope_ref, ssr_ref):
    @pl.when(pl.program_id(1) == 0)
    def _():
        x = ckv_ref[...].astype(F32)
        ms = jnp.mean(x * x, axis=-1, keepdims=True)
        xn_ref[...] = (x * lax.rsqrt(ms + EPS) * gkva_ref[...]).astype(xn_ref.dtype)
        kr = kr_ref[...].astype(F32)
        rope_ref[...] = _rope_mix(kr, gr_ref, cs_ref)
        ssr_ref[...] = _first_half_sumsq(kr)

    y = jnp.dot(xn_ref[...], w_ref[...], preferred_element_type=F32)
    kn = y[:, :QK_NOPE]
    ss = jnp.sum(kn * kn, axis=-1, keepdims=True) + ssr_ref[...]
    scale = lax.rsqrt(ss * (1.0 / QK_HEAD) + EPS)
    k_ref[:, :QK_NOPE] = (kn * gn_ref[...] * scale).astype(k_ref.dtype)
    k_ref[:, QK_NOPE:] = (rope_ref[:, :QK_ROPE] * scale).astype(k_ref.dtype)
    v_ref[...] = y[:, QK_NOPE:].astype(v_ref.dtype)


def _kv_prep(p, g_kva, w_kv, gn, gr, cs, *, tm):
    m = p.shape[0]
    return pl.pallas_call(
        _kv_prep_kernel,
        out_shape=(jax.ShapeDtypeStruct((N_HEADS, m, QK_HEAD), BF16),
                   jax.ShapeDtypeStruct((N_HEADS, m, V_HEAD), BF16)),
        grid=(m // tm, N_HEADS),
        in_specs=[
            pl.BlockSpec((tm, KV_LORA), lambda i, h: (i, P_KV // KV_LORA)),
            pl.BlockSpec((tm, 2 * QK_ROPE), lambda i, h: (i, P_KR // (2 * QK_ROPE))),
            pl.BlockSpec((1, KV_LORA), lambda i, h: (0, 0)),
            pl.BlockSpec((None, KV_LORA, 2 * LANE), lambda i, h: (h, 0, 0)),
            pl.BlockSpec((1, QK_NOPE), lambda i, h: (0, 0)),
            pl.BlockSpec((1, 2 * QK_ROPE), lambda i, h: (0, 0)),
            pl.BlockSpec((tm, 2 * QK_ROPE), lambda i, h: (i, 0)),
        ],
        out_specs=(pl.BlockSpec((None, tm, QK_HEAD), lambda i, h: (h, i, 0)),
                   pl.BlockSpec((None, tm, V_HEAD), lambda i, h: (h, i, 0))),
        scratch_shapes=[pltpu.VMEM((tm, KV_LORA), BF16),
                        pltpu.VMEM((tm, 2 * QK_ROPE), F32),
                        pltpu.VMEM((tm, 1), F32)],
        compiler_params=_params("parallel", "arbitrary"),
        name="kv_prep",
    )(p, p, g_kva, w_kv, gn, gr, cs)


def _attn_kernel(q_ref, k_ref, v_ref, o_ref):
    s = lax.dot_general(q_ref[...], k_ref[...], (((1,), (1,)), ((), ())),
                        preferred_element_type=F32)
    m = jnp.max(s, axis=-1, keepdims=True)
    e = jnp.exp(s - m)
    l = jnp.sum(e, axis=-1, keepdims=True)
    o = jnp.dot(e.astype(v_ref.dtype), v_ref[...], preferred_element_type=F32)
    o_ref[...] = (o / l).astype(o_ref.dtype)


def _attention(q, k, v, *, tq):
    _, m, _ = q.shape
    return pl.pallas_call(
        _attn_kernel,
        out_shape=jax.ShapeDtypeStruct((m, N_HEADS * V_HEAD), BF16),
        grid=(N_HEADS, m // tq),
        in_specs=[
            pl.BlockSpec((None, tq, QK_HEAD), lambda h, i: (h, i, 0)),
            pl.BlockSpec((None, m, QK_HEAD), lambda h, i: (h, 0, 0)),
            pl.BlockSpec((None, m, V_HEAD), lambda h, i: (h, 0, 0)),
        ],
        out_specs=pl.BlockSpec((tq, V_HEAD), lambda h, i: (i, h)),
        compiler_params=_params("parallel", "arbitrary"),
        name="attention",
    )(q, k, v)


def _dwconv3(u, w_ref):
    n = u.shape[0]
    row = lax.broadcasted_iota(jnp.int32, u.shape, 0)
    prev = jnp.where(row == 0, 0.0, pltpu.roll(u, 1, 0))
    nxt = jnp.where(row == n - 1, 0.0, pltpu.roll(u, n - 1, 0))
    return prev * w_ref[0:1, :] + u * w_ref[1:2, :] + nxt * w_ref[2:3, :]


def _short_conv_kernel(cb_ref, cc_ref, ch_ref, w_ref, o_ref):
    u = cc_ref[...].astype(F32) * ch_ref[...].astype(F32)
    o_ref[...] = (cb_ref[...].astype(F32) * _dwconv3(u, w_ref)).astype(o_ref.dtype)


def _short_conv(p, conv_w):
    m = p.shape[0]
    spec = lambda off: pl.BlockSpec((m, LANE), lambda j: (0, off // LANE + j))
    return pl.pallas_call(
        _short_conv_kernel,
        out_shape=jax.ShapeDtypeStruct((m, C_WIDTH), BF16),
        grid=(C_WIDTH // LANE,),
        in_specs=[spec(P_CB), spec(P_CC), spec(P_CH),
                  pl.BlockSpec((3, LANE), lambda j: (0, j))],
        out_specs=pl.BlockSpec((m, LANE), lambda j: (0, j)),
        compiler_params=_params("parallel"),
        name="short_conv",
    )(p, p, p, conv_w)


def _ffn_conv_kernel(a_ref, b_ref, wa_ref, wb_ref, o_ref):
    a = _dwconv3(a_ref[...].astype(F32), wa_ref)
    b = _dwconv3(b_ref[...].astype(F32), wb_ref)
    o_ref[...] = (a * jax.nn.sigmoid(a) * b).astype(o_ref.dtype)


def _ffn_conv(up, conv_w):
    m = up.shape[0]
    nb = D_FF // LANE
    return pl.pallas_call(
        _ffn_conv_kernel,
        out_shape=jax.ShapeDtypeStruct((m, D_FF), BF16),
        grid=(nb,),
        in_specs=[pl.BlockSpec((m, LANE), lambda j: (0, j)),
                  pl.BlockSpec((m, LANE), lambda j: (0, nb + j)),
                  pl.BlockSpec((3, LANE), lambda j: (0, j)),
                  pl.BlockSpec((3, LANE), lambda j: (0, nb + j))],
        out_specs=pl.BlockSpec((m, LANE), lambda j: (0, j)),
        compiler_params=_params("parallel"),
        name="ffn_conv",
    )(up, up, conv_w, conv_w)


def _merge_kernel(fa_ref, ob_ref, yc_ref, wa_ref, wb_ref, wc_ref, g0_ref, g1_ref, g2_ref, o_ref):
    def branch(x_ref, w_ref, g_ref):
        y = jnp.dot(x_ref[...], w_ref[...], preferred_element_type=F32)
        return jax.nn.sigmoid(g_ref[...].astype(F32)) * y

    merged = branch(fa_ref, wa_ref, g0_ref) + branch(ob_ref, wb_ref, g1_ref) + branch(yc_ref, wc_ref, g2_ref)
    o_ref[...] = merged.astype(o_ref.dtype)


def _merge(fa, ob, yc, w_pa, w_pb, w_pc, p, *, tm, tn):
    m = fa.shape[0]
    nj = D_MODEL // tn
    gate = lambda b: pl.BlockSpec((tm, tn), lambda i, j: (i, b * nj + j))
    return pl.pallas_call(
        _merge_kernel,
        out_shape=jax.ShapeDtypeStruct((m, D_MODEL), BF16),
        grid=(m // tm, nj),
        in_specs=[
            pl.BlockSpec((tm, F_WIDTH), lambda i, j: (i, 0)),
            pl.BlockSpec((tm, N_HEADS * V_HEAD), lambda i, j: (i, 0)),
            pl.BlockSpec((tm, C_WIDTH), lambda i, j: (i, 0)),
            pl.BlockSpec((F_WIDTH, tn), lambda i, j: (0, j)),
            pl.BlockSpec((N_HEADS * V_HEAD, tn), lambda i, j: (0, j)),
            pl.BlockSpec((C_WIDTH, tn), lambda i, j: (0, j)),
            gate(0), gate(1), gate(2),
        ],
        out_specs=pl.BlockSpec((tm, tn), lambda i, j: (i, j)),
        compiler_params=_params("parallel", "arbitrary"),
        name="merge",
    )(fa, ob, yc, w_pa, w_pb, w_pc, p, p, p)


def _matmul_residual_kernel(x_ref, w_ref, h_ref, o_ref):
    o_ref[...] = h_ref[...] + jnp.dot(x_ref[...], w_ref[...], preferred_element_type=F32)


def _matmul_residual(x, w, h, *, tm, tn):
    m, k = x.shape
    n = w.shape[1]
    return pl.pallas_call(
        _matmul_residual_kernel,
        out_shape=jax.ShapeDtypeStruct((m, n), F32),
        grid=(m // tm, n // tn),
        in_specs=[
            pl.BlockSpec((tm, k), lambda i, j: (i, 0)),
            pl.BlockSpec((k, tn), lambda i, j: (0, j)),
            pl.BlockSpec((tm, tn), lambda i, j: (i, j)),
        ],
        out_specs=pl.BlockSpec((tm, tn), lambda i, j: (i, j)),
        compiler_params=_params("parallel", "arbitrary"),
        name="matmul_residual",
    )(x, w, h)


def _rotate_half_cols(w):
    half = QK_ROPE // 2
    return jnp.concatenate([-w[..., half:], w[..., :half]], axis=-1)


def _swap_halves(g):
    half = QK_ROPE // 2
    return jnp.concatenate([g[..., half:], g[..., :half]], axis=-1)


def _rope_table(n):
    inv = 1.0 / (ROPE_THETA ** (jnp.arange(0, QK_ROPE, 2, dtype=F32) / QK_ROPE))
    ang = jnp.arange(n, dtype=F32)[:, None] * inv[None, :]
    cos, sin = jnp.cos(ang), jnp.sin(ang)
    return jnp.concatenate([cos, cos, sin, sin], axis=-1)


def _chan_dft_table():
    idx = np.arange(F_GROUP)
    ang = 2.0 * np.pi * ((idx[:, None] * idx[None, :]) % F_GROUP) / F_GROUP
    scale = 1.0 / math.sqrt(F_GROUP)
    return jnp.asarray(np.concatenate([np.cos(ang), np.sin(ang)], axis=1) * scale, dtype=BF16)


def _pos_dft_table(n, n_pad):
    row = jnp.arange(n, dtype=jnp.int32)[:, None]
    col = jnp.arange(n_pad, dtype=jnp.int32)[None, :]
    ang = ((row * col) % n).astype(F32) * (2.0 * math.pi / n)
    scale = jnp.where(col < n, 1.0 / math.sqrt(n), 0.0)
    return jnp.concatenate([jnp.cos(ang) * scale, jnp.sin(ang) * (-scale)], axis=1).astype(BF16)


def _in_proj_weight(w):
    kr = w[:, OFF_KR:OFF_C]
    return jnp.concatenate([
        w[:, OFF_G:], w[:, :OFF_Q], w[:, OFF_C:OFF_G], w[:, OFF_KV:OFF_KR], w[:, OFF_Q:OFF_KV],
        kr, _rotate_half_cols(kr)], axis=1).astype(BF16)


def _q_weight(w):
    w = w.reshape(Q_LORA, N_HEADS, QK_HEAD)
    rope = w[..., QK_NOPE:]
    w = jnp.concatenate([w[..., :QK_NOPE], rope, _rotate_half_cols(rope)], axis=-1)
    return w.transpose(1, 0, 2).astype(BF16)


def _kv_weight(w):
    return w.reshape(KV_LORA, N_HEADS, QK_NOPE + V_HEAD).transpose(1, 0, 2).astype(BF16)


def _qk_gains(g):
    rope = g[QK_NOPE:]
    return g[None, :QK_NOPE], jnp.concatenate([rope, _swap_halves(rope)])[None, :]


def kernel(x, meta_tokens, g_mix, w_in, g_qa, g_kva, w_uq, w_ukv, g_q, g_k, conv_c,
           w_pa, w_pb, w_pc, w_o, g_ffn, w_up, conv_ffn, w_down):
    assert x.shape[0] == 1 and x.shape[2] == D_MODEL
    depth = w_in.shape[0]
    h = jnp.concatenate([meta_tokens.astype(x.dtype), x[0]], axis=0)
    n = h.shape[0]

    tm = _row_tile(n, 912)
    tq = _row_tile(n, 432)
    tn_in = _row_tile(P_N, 1024, LANE)
    tn_up = _row_tile(2 * D_FF, 1024, LANE)

    cs = _rope_table(n)
    w_cs = _chan_dft_table()
    n_pad = -(-n // MXU_DIM) * MXU_DIM
    tk = _row_tile(n_pad, 3072, MXU_DIM)
    t_pos = _pos_dft_table(n, n_pad)

    for l in range(depth):
        p = _rms_matmul(h, g_mix[l][None], _in_proj_weight(w_in[l]), tm=tm, tn=tn_in)

        u = _chan_dft(p, w_cs, tm=tm, m_pad=n_pad).reshape(2 * n_pad, F_WIDTH)
        fa = _matmul_acc(t_pos, u, tm=tm, tk=tk)

        gqn, gqr = _qk_gains(g_q[l])
        gkn, gkr = _qk_gains(g_k[l])
        q = _q_prep(p, g_qa[l][None], _q_weight(w_uq[l]), gqn, gqr, cs, tm=tm)
        k, v = _kv_prep(p, g_kva[l][None], _kv_weight(w_ukv[l]), gkn, gkr, cs, tm=tm)
        ob = _attention(q, k, v, tq=tq)

        yc = _short_conv(p, conv_c[l])

        merged = _merge(fa, ob, yc, w_pa[l].astype(BF16), w_pb[l].astype(BF16), w_pc[l].astype(BF16),
                        p, tm=tm, tn=512)
        h = _matmul_residual(merged, w_o[l].astype(BF16), h, tm=tm, tn=512)

        up = _rms_matmul(h, g_ffn[l][None], w_up[l].astype(BF16), tm=tm, tn=tn_up)
        act = _ffn_conv(up, conv_ffn[l])
        h = _matmul_residual(act, w_down[l].astype(BF16), h, tm=tm, tn=512)

    return h[None, N_META:]
```

```python
import functools
import math

import jax
import jax.numpy as jnp
import numpy as np
from jax import lax
from jax.experimental import pallas as pl
from jax.experimental.pallas import tpu as pltpu

D_MODEL = 2048
N_META = 16
F_WIDTH = D_MODEL // 2
F_GROUPS = 4
F_GROUP = F_WIDTH // F_GROUPS
N_HEADS = 16
QK_NOPE = 128
QK_ROPE = 64
QK_HEAD = QK_NOPE + QK_ROPE
V_HEAD = 128
Q_LORA = 768
KV_LORA = 512
ROPE_THETA = 10000.0
C_WIDTH = D_MODEL // 2
N_BRANCH = 3
D_FF = 5632
EPS = 1e-6

OFF_Q = F_WIDTH
OFF_KV = OFF_Q + Q_LORA
OFF_KR = OFF_KV + KV_LORA
OFF_C = OFF_KR + QK_ROPE
OFF_G = OFF_C + 3 * C_WIDTH

LANE = 128
HEAD_COLS = 2 * LANE
P_G = 0
P_A = P_G + N_BRANCH * D_MODEL
P_CB = P_A + F_WIDTH
P_CC = P_CB + C_WIDTH
P_CH = P_CC + C_WIDTH
P_KV = P_CH + C_WIDTH
P_Q = P_KV + KV_LORA
P_KR = P_Q + Q_LORA
P_N = P_KR + 2 * QK_ROPE

SCORE_SCALE = math.log2(math.e) / math.sqrt(QK_HEAD)

VMEM_LIMIT_BYTES = 56 * 1024 * 1024

BF16 = jnp.bfloat16
F32 = jnp.float32


def _params(*semantics):
    return pltpu.CompilerParams(dimension_semantics=semantics, vmem_limit_bytes=VMEM_LIMIT_BYTES)


def _row_tile(n, target, mult=16):
    best = None
    for d in range(mult, min(n, target) + 1, mult):
        if n % d == 0:
            best = d
    assert best is not None, (n, target, mult)
    return best


def _rms_matmul_kernel(x_ref, g_ref, w_ref, o_ref, xn_ref):
    @pl.when(pl.program_id(1) == 0)
    def _():
        x = x_ref[...].astype(F32)
        ms = jnp.mean(x * x, axis=-1, keepdims=True)
        xn_ref[...] = (x * lax.rsqrt(ms + EPS) * g_ref[...]).astype(xn_ref.dtype)

    o_ref[...] = jnp.dot(xn_ref[...], w_ref[...], preferred_element_type=F32).astype(o_ref.dtype)


def _rms_matmul(x, g, w, *, tm, tn, out_dtype=BF16):
    m, k = x.shape
    n = w.shape[1]
    return pl.pallas_call(
        _rms_matmul_kernel,
        out_shape=jax.ShapeDtypeStruct((m, n), out_dtype),
        grid=(m // tm, n // tn),
        in_specs=[
            pl.BlockSpec((tm, k), lambda i, j: (i, 0)),
            pl.BlockSpec((1, k), lambda i, j: (0, 0)),
            pl.BlockSpec((k, tn), lambda i, j: (0, j)),
        ],
        out_specs=pl.BlockSpec((tm, tn), lambda i, j: (i, j)),
        scratch_shapes=[pltpu.VMEM((tm, k), BF16)],
        compiler_params=_params("parallel", "arbitrary"),
        name="rms_matmul",
    )(x, g, w)


def _chan_dft_kernel(a_ref, w_ref, o_ref):
    y = jnp.dot(a_ref[...], w_ref[...].astype(BF16), preferred_element_type=F32)
    o_ref[0] = y[:, :F_GROUP].astype(o_ref.dtype)
    o_ref[1] = y[:, F_GROUP:].astype(o_ref.dtype)


def _chan_dft(p, w_cs, *, tm):
    m = p.shape[0]
    base = P_A // F_GROUP
    return pl.pallas_call(
        _chan_dft_kernel,
        out_shape=jax.ShapeDtypeStruct((2, m, F_WIDTH), BF16),
        grid=(m // tm, F_GROUPS),
        in_specs=[
            pl.BlockSpec((tm, F_GROUP), lambda i, g: (i, base + g)),
            pl.BlockSpec((F_GROUP, 2 * F_GROUP), lambda i, g: (0, 0)),
        ],
        out_specs=pl.BlockSpec((2, tm, F_GROUP), lambda i, g: (0, i, g)),
        compiler_params=_params("parallel", "arbitrary"),
        name="chan_dft",
    )(p, w_cs)


def _pos_dft_inner_kernel(u_ref, m_ref, o_ref):
    _, n1, tc = u_ref.shape
    _, nb, _, c = o_ref.shape
    y = jnp.dot(m_ref[...].astype(BF16), u_ref[...].reshape(2 * n1, tc), preferred_element_type=F32)
    for b in range(nb):
        o_ref[0, b] = y[:n1, b * c:(b + 1) * c].astype(o_ref.dtype)
        o_ref[1, b] = y[n1:, b * c:(b + 1) * c].astype(o_ref.dtype)


def _pos_dft_inner(u, m1, *, n1, n2):
    c = u.shape[2] // n2
    nb = _row_tile(n2, 9, 1)
    return pl.pallas_call(
        _pos_dft_inner_kernel,
        out_shape=jax.ShapeDtypeStruct((2, n2, n1, c), BF16),
        grid=(n2 // nb,),
        in_specs=[pl.BlockSpec((2, n1, nb * c), lambda j: (0, 0, j)),
                  pl.BlockSpec((2 * n1, 2 * n1), lambda j: (0, 0))],
        out_specs=pl.BlockSpec((2, nb, n1, c), lambda j: (0, j, 0, 0)),
        compiler_params=_params("parallel"),
        name="pos_dft_inner",
    )(u, m1)


def _pos_dft_outer_kernel(y_ref, twc_ref, tws_ref, c2_ref, s2_ref, o_ref):
    yr = y_ref[0].astype(F32)
    yi = y_ref[1].astype(F32)
    twc = twc_ref[...]
    tws = tws_ref[...]
    zr = (yr * twc + yi * tws).astype(BF16)
    zi = (yi * twc - yr * tws).astype(BF16)
    x = (jnp.dot(c2_ref[...].astype(BF16), zr, preferred_element_type=F32)
         + jnp.dot(s2_ref[...].astype(BF16), zi, preferred_element_type=F32))
    o_ref[...] = x.astype(o_ref.dtype)


def _pos_dft_outer(y, twc, tws, c2, s2, *, c):
    _, n2, width = y.shape
    return pl.pallas_call(
        _pos_dft_outer_kernel,
        out_shape=jax.ShapeDtypeStruct((n2, width), BF16),
        grid=(width // c,),
        in_specs=[pl.BlockSpec((2, n2, c), lambda k1: (0, 0, k1)),
                  pl.BlockSpec((None, n2, 1), lambda k1: (k1, 0, 0)),
                  pl.BlockSpec((None, n2, 1), lambda k1: (k1, 0, 0)),
                  pl.BlockSpec((n2, n2), lambda k1: (0, 0)),
                  pl.BlockSpec((n2, n2), lambda k1: (0, 0))],
        out_specs=pl.BlockSpec((n2, c), lambda k1: (0, k1)),
        compiler_params=_params("parallel"),
        name="pos_dft_outer",
    )(y, twc, tws, c2, s2)


def _rope_mix(blk, gcs):
    t = blk * gcs
    return t + pltpu.roll(t, QK_ROPE, 1)


def _first_half_mask():
    lane = lax.broadcasted_iota(jnp.int32, (1, 2 * QK_ROPE), 1)
    return jnp.where(lane < QK_ROPE, 1.0, 0.0).astype(F32)


def _latent_norm(x_ref, g_ref):
    x = x_ref[...].astype(F32)
    ms = jnp.mean(x * x, axis=-1, keepdims=True)
    return (x * lax.rsqrt(ms + EPS) * g_ref[...]).astype(BF16)


def _q_prep_kernel(cq_ref, gqa_ref, w_ref, gn_ref, gr_ref, cs_ref, o_ref, *, n_valid_tiles):
    @pl.when(pl.program_id(0) >= n_valid_tiles)
    def _():
        o_ref[...] = jnp.zeros_like(o_ref)

    pl.when(pl.program_id(0) < n_valid_tiles)(
        functools.partial(_q_prep_tile, cq_ref, gqa_ref, w_ref, gn_ref, gr_ref, cs_ref, o_ref))


def _q_prep_tile(cq_ref, gqa_ref, w_ref, gn_ref, gr_ref, cs_ref, o_ref):
    xn = _latent_norm(cq_ref, gqa_ref)
    gcs = gr_ref[...] * cs_ref[...]
    gn = gn_ref[...]
    half = _first_half_mask()
    for h in range(N_HEADS):
        y = jnp.dot(xn, w_ref[:, h * HEAD_COLS:(h + 1) * HEAD_COLS], preferred_element_type=F32)
        y0 = y[:, :QK_NOPE]
        y1 = y[:, QK_NOPE:]
        ss = jnp.sum(y0 * y0 + y1 * y1 * half, axis=-1, keepdims=True)
        scale = lax.rsqrt(ss * (1.0 / QK_HEAD) + EPS) * SCORE_SCALE
        r = _rope_mix(y1, gcs)
        o_ref[h, :, :QK_NOPE] = (y0 * gn * scale).astype(o_ref.dtype)
        o_ref[h, :, QK_NOPE:] = (r[:, :QK_ROPE] * scale).astype(o_ref.dtype)


def _q_prep(p, g_qa, w_q, gn, gr, cs, *, tm, m_pad):
    m = p.shape[0]
    nt = m // tm
    row = lambda i: jnp.minimum(i, nt - 1)
    return pl.pallas_call(
        functools.partial(_q_prep_kernel, n_valid_tiles=nt),
        out_shape=jax.ShapeDtypeStruct((N_HEADS, m_pad, QK_HEAD), BF16),
        grid=(pl.cdiv(m_pad, tm),),
        in_specs=[
            pl.BlockSpec((tm, Q_LORA), lambda i: (row(i), P_Q // Q_LORA)),
            pl.BlockSpec((1, Q_LORA), lambda i: (0, 0)),
            pl.BlockSpec((Q_LORA, N_HEADS * HEAD_COLS), lambda i: (0, 0)),
            pl.BlockSpec((1, QK_NOPE), lambda i: (0, 0)),
            pl.BlockSpec((1, 2 * QK_ROPE), lambda i: (0, 0)),
            pl.BlockSpec((tm, 2 * QK_ROPE), lambda i: (row(i), 0)),
        ],
        out_specs=pl.BlockSpec((N_HEADS, tm, QK_HEAD), lambda i: (0, i, 0)),
        compiler_params=_params("parallel"),
        name="q_prep",
    )(p, g_qa, w_q, gn, gr, cs)


def _kv_prep_kernel(ckv_ref, kr_ref, gkva_ref, w_ref, gn_ref, gr_ref, cs_ref, k_ref, v_ref):
    xn = _latent_norm(ckv_ref, gkva_ref)
    kr = kr_ref[...].astype(F32)
    rope = _rope_mix(kr, gr_ref[...] * cs_ref[...])[:, :QK_ROPE]
    ss_rope = jnp.sum(kr * kr * _first_half_mask(), axis=-1, keepdims=True)
    gn = gn_ref[...]
    for h in range(N_HEADS):
        y = jnp.dot(xn, w_ref[:, h * HEAD_COLS:(h + 1) * HEAD_COLS], preferred_element_type=F32)
        kn = y[:, :QK_NOPE]
        ss = jnp.sum(kn * kn, axis=-1, keepdims=True) + ss_rope
        scale = lax.rsqrt(ss * (1.0 / QK_HEAD) + EPS)
        k_ref[h, :, :QK_NOPE] = (kn * gn * scale).astype(k_ref.dtype)
        k_ref[h, :, QK_NOPE:] = (rope * scale).astype(k_ref.dtype)
        v_ref[h] = y[:, QK_NOPE:].astype(v_ref.dtype)


def _kv_prep(p, g_kva, w_kv, gn, gr, cs, *, tm):
    m = p.shape[0]
    return pl.pallas_call(
        _kv_prep_kernel,
        out_shape=(jax.ShapeDtypeStruct((N_HEADS, m, QK_HEAD), BF16),
                   jax.ShapeDtypeStruct((N_HEADS, m, V_HEAD), BF16)),
        grid=(m // tm,),
        in_specs=[
            pl.BlockSpec((tm, KV_LORA), lambda i: (i, P_KV // KV_LORA)),
            pl.BlockSpec((tm, 2 * QK_ROPE), lambda i: (i, P_KR // (2 * QK_ROPE))),
            pl.BlockSpec((1, KV_LORA), lambda i: (0, 0)),
            pl.BlockSpec((KV_LORA, N_HEADS * HEAD_COLS), lambda i: (0, 0)),
            pl.BlockSpec((1, QK_NOPE), lambda i: (0, 0)),
            pl.BlockSpec((1, 2 * QK_ROPE), lambda i: (0, 0)),
            pl.BlockSpec((tm, 2 * QK_ROPE), lambda i: (i, 0)),
        ],
        out_specs=(pl.BlockSpec((N_HEADS, tm, QK_HEAD), lambda i: (0, i, 0)),
                   pl.BlockSpec((N_HEADS, tm, V_HEAD), lambda i: (0, i, 0))),
        compiler_params=_params("parallel"),
        name="kv_prep",
    )(p, p, g_kva, w_kv, gn, gr, cs)


def _attn_kernel(q_ref, k_ref, v_ref, o_ref, *, kc):
    q = q_ref[...]
    n = k_ref.shape[0]
    bounds = [(lo, min(lo + kc, n)) for lo in range(0, n, kc)]
    nc = len(bounds)

    def scores(c):
        lo, hi = bounds[c]
        return lax.dot_general(k_ref[lo:hi, :], q, (((1,), (1,)), ((), ())),
                               preferred_element_type=F32)

    s_next = scores(0)
    m = l = acc = None
    for c in range(nc):
        lo, hi = bounds[c]
        s = s_next
        if c + 1 < nc:
            s_next = scores(c + 1)
        m_c = jnp.max(s, axis=0, keepdims=True)
        m_new = m_c if c == 0 else jnp.maximum(m, m_c)
        e = jnp.exp2(s - m_new)
        l_c = jnp.sum(e, axis=0, keepdims=True)
        pv = lax.dot_general(v_ref[lo:hi, :], e.astype(v_ref.dtype),
                             (((0,), (0,)), ((), ())), preferred_element_type=F32)
        if c == 0:
            l, acc = l_c, pv
        else:
            alpha = jnp.exp2(m - m_new)
            l = alpha * l + l_c
            acc = alpha * acc + pv
        m = m_new
    o_ref[...] = (acc / l).T.astype(o_ref.dtype)


def _attention(q, k, v, *, tq, kc):
    _, m, _ = k.shape
    assert q.shape[1] % tq == 0
    return pl.pallas_call(
        functools.partial(_attn_kernel, kc=kc),
        out_shape=jax.ShapeDtypeStruct((m, N_HEADS * V_HEAD), BF16),
        grid=(N_HEADS, q.shape[1] // tq),
        in_specs=[
            pl.BlockSpec((None, tq, QK_HEAD), lambda h, i: (h, i, 0)),
            pl.BlockSpec((None, m, QK_HEAD), lambda h, i: (h, 0, 0)),
            pl.BlockSpec((None, m, V_HEAD), lambda h, i: (h, 0, 0)),
        ],
        out_specs=pl.BlockSpec((tq, V_HEAD), lambda h, i: (i, h)),
        compiler_params=_params("parallel", "arbitrary"),
        name="attention",
    )(q, k, v)


def _dwconv3(u, w_ref):
    n = u.shape[0]
    row = lax.broadcasted_iota(jnp.int32, u.shape, 0)
    prev = jnp.where(row == 0, 0.0, pltpu.roll(u, 1, 0))
    nxt = jnp.where(row == n - 1, 0.0, pltpu.roll(u, n - 1, 0))
    return prev * w_ref[0:1, :] + u * w_ref[1:2, :] + nxt * w_ref[2:3, :]


def _short_conv_kernel(cb_ref, cc_ref, ch_ref, w_ref, o_ref):
    u = cc_ref[...].astype(F32) * ch_ref[...].astype(F32)
    o_ref[...] = (cb_ref[...].astype(F32) * _dwconv3(u, w_ref)).astype(o_ref.dtype)


def _short_conv(p, conv_w):
    m = p.shape[0]
    spec = lambda off: pl.BlockSpec((m, LANE), lambda j: (0, off // LANE + j))
    return pl.pallas_call(
        _short_conv_kernel,
        out_shape=jax.ShapeDtypeStruct((m, C_WIDTH), BF16),
        grid=(C_WIDTH // LANE,),
        in_specs=[spec(P_CB), spec(P_CC), spec(P_CH),
                  pl.BlockSpec((3, LANE), lambda j: (0, j))],
        out_specs=pl.BlockSpec((m, LANE), lambda j: (0, j)),
        compiler_params=_params("parallel"),
        name="short_conv",
    )(p, p, p, conv_w)


def _ffn_conv_kernel(a_ref, b_ref, wa_ref, wb_ref, o_ref):
    a = _dwconv3(a_ref[...].astype(F32), wa_ref)
    b = _dwconv3(b_ref[...].astype(F32), wb_ref)
    o_ref[...] = (a * jax.nn.sigmoid(a) * b).astype(o_ref.dtype)


def _ffn_conv(up, conv_w):
    m = up.shape[0]
    nb = D_FF // LANE
    return pl.pallas_call(
        _ffn_conv_kernel,
        out_shape=jax.ShapeDtypeStruct((m, D_FF), BF16),
        grid=(nb,),
        in_specs=[pl.BlockSpec((m, LANE), lambda j: (0, j)),
                  pl.BlockSpec((m, LANE), lambda j: (0, nb + j)),
                  pl.BlockSpec((3, LANE), lambda j: (0, j)),
                  pl.BlockSpec((3, LANE), lambda j: (0, nb + j))],
        out_specs=pl.BlockSpec((m, LANE), lambda j: (0, j)),
        compiler_params=_params("parallel"),
        name="ffn_conv",
    )(up, up, conv_w, conv_w)


def _merge_kernel(fa_ref, ob_ref, yc_ref, wa_ref, wb_ref, wc_ref, g0_ref, g1_ref, g2_ref, o_ref):
    def branch(x_ref, w_ref, g_ref):
        y = jnp.dot(x_ref[...], w_ref[...], preferred_element_type=F32)
        return jax.nn.sigmoid(g_ref[...].astype(F32)) * y

    merged = branch(fa_ref, wa_ref, g0_ref) + branch(ob_ref, wb_ref, g1_ref) + branch(yc_ref, wc_ref, g2_ref)
    o_ref[...] = merged.astype(o_ref.dtype)


def _merge(fa, ob, yc, w_pa, w_pb, w_pc, p, *, tm, tn):
    m = fa.shape[0]
    nj = D_MODEL // tn
    gate = lambda b: pl.BlockSpec((tm, tn), lambda i, j: (i, b * nj + j))
    return pl.pallas_call(
        _merge_kernel,
        out_shape=jax.ShapeDtypeStruct((m, D_MODEL), BF16),
        grid=(m // tm, nj),
        in_specs=[
            pl.BlockSpec((tm, F_WIDTH), lambda i, j: (i, 0)),
            pl.BlockSpec((tm, N_HEADS * V_HEAD), lambda i, j: (i, 0)),
            pl.BlockSpec((tm, C_WIDTH), lambda i, j: (i, 0)),
            pl.BlockSpec((F_WIDTH, tn), lambda i, j: (0, j)),
            pl.BlockSpec((N_HEADS * V_HEAD, tn), lambda i, j: (0, j)),
            pl.BlockSpec((C_WIDTH, tn), lambda i, j: (0, j)),
            gate(0), gate(1), gate(2),
        ],
        out_specs=pl.BlockSpec((tm, tn), lambda i, j: (i, j)),
        compiler_params=_params("parallel", "arbitrary"),
        name="merge",
    )(fa, ob, yc, w_pa, w_pb, w_pc, p, p, p)


def _matmul_residual_kernel(x_ref, w_ref, h_ref, o_ref):
    o_ref[...] = h_ref[...] + jnp.dot(x_ref[...], w_ref[...], preferred_element_type=F32)


def _matmul_residual(x, w, h, *, tm, tn):
    m, k = x.shape
    n = w.shape[1]
    return pl.pallas_call(
        _matmul_residual_kernel,
        out_shape=jax.ShapeDtypeStruct((m, n), F32),
        grid=(m // tm, n // tn),
        in_specs=[
            pl.BlockSpec((tm, k), lambda i, j: (i, 0)),
            pl.BlockSpec((k, tn), lambda i, j: (0, j)),
            pl.BlockSpec((tm, tn), lambda i, j: (i, j)),
        ],
        out_specs=pl.BlockSpec((tm, tn), lambda i, j: (i, j)),
        compiler_params=_params("parallel", "arbitrary"),
        name="matmul_residual",
    )(x, w, h)


def _rotate_half_cols(w):
    half = QK_ROPE // 2
    return jnp.concatenate([-w[..., half:], w[..., :half]], axis=-1)


def _swap_halves(g):
    half = QK_ROPE // 2
    return jnp.concatenate([g[..., half:], g[..., :half]], axis=-1)


def _rope_table(n):
    inv = 1.0 / (ROPE_THETA ** (jnp.arange(0, QK_ROPE, 2, dtype=F32) / QK_ROPE))
    ang = jnp.arange(n, dtype=F32)[:, None] * inv[None, :]
    cos, sin = jnp.cos(ang), jnp.sin(ang)
    return jnp.concatenate([cos, cos, sin, sin], axis=-1)


def _dft_cos_sin(rows, cols, period):
    ang = 2.0 * np.pi * ((rows[:, None] * cols[None, :]) % period) / period
    return np.cos(ang), np.sin(ang)


def _chan_dft_table():
    idx = np.arange(F_GROUP)
    c, s = _dft_cos_sin(idx, idx, F_GROUP)
    return jnp.asarray(np.concatenate([c, -s], axis=1) / math.sqrt(F_GROUP), dtype=F32)


def _pos_dft_tables(n, n1, n2):
    i1, i2 = np.arange(n1), np.arange(n2)
    c1, s1 = _dft_cos_sin(i1, i1, n1)
    m1 = np.block([[c1, s1], [-s1, c1]])
    twc, tws = _dft_cos_sin(i1, i2, n)
    c2, s2 = _dft_cos_sin(i2, i2, n2)
    scale = 1.0 / math.sqrt(n)
    return tuple(jnp.asarray(t, dtype=F32)
                 for t in (m1, twc[:, :, None], tws[:, :, None], c2 * scale, s2 * scale))


def _in_proj_weight(w):
    kr = w[:, OFF_KR:OFF_C]
    return jnp.concatenate([
        w[:, OFF_G:], w[:, :OFF_Q], w[:, OFF_C:OFF_G], w[:, OFF_KV:OFF_KR], w[:, OFF_Q:OFF_KV],
        kr, _rotate_half_cols(kr)], axis=1).astype(BF16)


def _q_weight(w):
    w = w.reshape(Q_LORA, N_HEADS, QK_HEAD)
    rope = w[..., QK_NOPE:]
    w = jnp.concatenate([w[..., :QK_NOPE], rope, _rotate_half_cols(rope)], axis=-1)
    return w.reshape(Q_LORA, N_HEADS * HEAD_COLS).astype(BF16)


def _qk_gains(g):
    rope = g[QK_NOPE:]
    return g[None, :QK_NOPE], jnp.concatenate([rope, _swap_halves(rope)])[None, :]


def kernel(x, meta_tokens, g_mix, w_in, g_qa, g_kva, w_uq, w_ukv, g_q, g_k, conv_c,
           w_pa, w_pb, w_pc, w_o, g_ffn, w_up, conv_ffn, w_down):
    assert x.shape[0] == 1 and x.shape[2] == D_MODEL
    depth = w_in.shape[0]
    h = jnp.concatenate([meta_tokens.astype(x.dtype), x[0]], axis=0)
    n = h.shape[0]

    tm = _row_tile(n, 912)
    tp = _row_tile(n, 144)
    tq = 4 * LANE
    kc = _row_tile(n, 1368, 8)
    tn_in = _row_tile(P_N, 1664, LANE)
    tn_up = _row_tile(2 * D_FF, 1024, LANE)

    cs = _rope_table(n)
    w_cs = _chan_dft_table()
    n1 = _row_tile(n, 64)
    n2 = n // n1
    m1, twc, tws, c2, s2 = _pos_dft_tables(n, n1, n2)

    for l in range(depth):
        p = _rms_matmul(h, g_mix[l][None], _in_proj_weight(w_in[l]), tm=tm, tn=tn_in)

        u = _chan_dft(p, w_cs, tm=tm).reshape(2, n1, n2 * F_WIDTH)
        y = _pos_dft_inner(u, m1, n1=n1, n2=n2).reshape(2, n2, n1 * F_WIDTH)
        fa = _pos_dft_outer(y, twc, tws, c2, s2, c=F_WIDTH).reshape(n, F_WIDTH)

        gqn, gqr = _qk_gains(g_q[l])
        gkn, gkr = _qk_gains(g_k[l])
        q = _q_prep(p, g_qa[l][None], _q_weight(w_uq[l]), gqn, gqr, cs, tm=tp, m_pad=-(-n // tq) * tq)
        k, v = _kv_prep(p, g_kva[l][None], w_ukv[l].astype(BF16), gkn, gkr, cs, tm=tp)
        ob = _attention(q, k, v, tq=tq, kc=kc)

        yc = _short_conv(p, conv_c[l])

        merged = _merge(fa, ob, yc, w_pa[l].astype(BF16), w_pb[l].astype(BF16), w_pc[l].astype(BF16),
                        p, tm=tm, tn=512)
        h = _matmul_residual(merged, w_o[l].astype(BF16), h, tm=tm, tn=512)

        up = _rms_matmul(h, g_ffn[l][None], w_up[l].astype(BF16), tm=tm, tn=tn_up)
        act = _ffn_conv(up, conv_ffn[l])
        h = _matmul_residual(act, w_down[l].astype(BF16), h, tm=tm, tn=512)

    return h[None, N_META:]
```

```python
import functools
import math

import jax
import jax.numpy as jnp
import numpy as np
from jax import lax
from jax.experimental import pallas as pl
from jax.experimental.pallas import tpu as pltpu

D_MODEL = 2048
N_META = 16
F_WIDTH = D_MODEL // 2
F_GROUPS = 4
F_GROUP = F_WIDTH // F_GROUPS
N_HEADS = 16
QK_NOPE = 128
QK_ROPE = 64
QK_HEAD = QK_NOPE + QK_ROPE
V_HEAD = 128
Q_LORA = 768
KV_LORA = 512
ROPE_THETA = 10000.0
C_WIDTH = D_MODEL // 2
N_BRANCH = 3
D_FF = 5632
EPS = 1e-6

OFF_Q = F_WIDTH
OFF_KV = OFF_Q + Q_LORA
OFF_KR = OFF_KV + KV_LORA
OFF_C = OFF_KR + QK_ROPE
OFF_G = OFF_C + 3 * C_WIDTH

LANE = 128
HEAD_COLS = 2 * LANE
P_G = 0
P_A = P_G + N_BRANCH * D_MODEL
P_CB = P_A + F_WIDTH
P_CC = P_CB + C_WIDTH
P_CH = P_CC + C_WIDTH
P_KV = P_CH + C_WIDTH
P_Q = P_KV + KV_LORA
P_KR = P_Q + Q_LORA
P_N = P_KR + 2 * QK_ROPE

SCORE_SCALE = math.log2(math.e) / math.sqrt(QK_HEAD)

VMEM_LIMIT_BYTES = 56 * 1024 * 1024

BF16 = jnp.bfloat16
F32 = jnp.float32


def _params(*semantics):
    return pltpu.CompilerParams(dimension_semantics=semantics, vmem_limit_bytes=VMEM_LIMIT_BYTES)


def _row_tile(n, target, mult=16):
    best = None
    for d in range(mult, min(n, target) + 1, mult):
        if n % d == 0:
            best = d
    assert best is not None, (n, target, mult)
    return best


def _rms_matmul_kernel(x_ref, g_ref, w_ref, o_ref, xn_ref):
    @pl.when(pl.program_id(1) == 0)
    def _():
        x = x_ref[...].astype(F32)
        ms = jnp.mean(x * x, axis=-1, keepdims=True)
        xn_ref[...] = (x * lax.rsqrt(ms + EPS) * g_ref[...]).astype(xn_ref.dtype)

    o_ref[...] = jnp.dot(xn_ref[...], w_ref[...], preferred_element_type=F32).astype(o_ref.dtype)


def _rms_matmul(x, g, w, *, tm, tn, out_dtype=BF16):
    m, k = x.shape
    n = w.shape[1]
    return pl.pallas_call(
        _rms_matmul_kernel,
        out_shape=jax.ShapeDtypeStruct((m, n), out_dtype),
        grid=(m // tm, n // tn),
        in_specs=[
            pl.BlockSpec((tm, k), lambda i, j: (i, 0)),
            pl.BlockSpec((1, k), lambda i, j: (0, 0)),
            pl.BlockSpec((k, tn), lambda i, j: (0, j)),
        ],
        out_specs=pl.BlockSpec((tm, tn), lambda i, j: (i, j)),
        scratch_shapes=[pltpu.VMEM((tm, k), BF16)],
        compiler_params=_params("parallel", "arbitrary"),
        name="rms_matmul",
    )(x, g, w)


SLABS_PER_GROUP = F_GROUP // LANE


def _chan_dft_kernel(a_ref, w_ref, o_ref):
    y = jnp.dot(a_ref[...], w_ref[...].astype(BF16), preferred_element_type=F32)
    for s in range(SLABS_PER_GROUP):
        o_ref[s, 0] = y[:, s * LANE:(s + 1) * LANE]
        o_ref[s, 1] = y[:, F_GROUP + s * LANE:F_GROUP + (s + 1) * LANE]


def _chan_dft(p, w_cs, *, tm):
    m = p.shape[0]
    base = P_A // F_GROUP
    return pl.pallas_call(
        _chan_dft_kernel,
        out_shape=jax.ShapeDtypeStruct((F_WIDTH // LANE, 2, m, LANE), F32),
        grid=(m // tm, F_GROUPS),
        in_specs=[
            pl.BlockSpec((tm, F_GROUP), lambda i, g: (i, base + g)),
            pl.BlockSpec((F_GROUP, 2 * F_GROUP), lambda i, g: (0, 0)),
        ],
        out_specs=pl.BlockSpec((SLABS_PER_GROUP, 2, tm, LANE), lambda i, g: (g, 0, i, 0)),
        compiler_params=_params("parallel", "arbitrary"),
        name="chan_dft",
    )(p, w_cs)


def _pos_dft_kernel(u_ref, m1_ref, twc_ref, tws_ref, c2_ref, s2_ref, o_ref, y_ref, *, n1, n2):
    m1 = m1_ref[...].astype(BF16)
    for j2 in range(n2):
        a = jnp.concatenate([u_ref[0, pl.ds(j2, n1, stride=n2), :],
                             u_ref[1, pl.ds(j2, n1, stride=n2), :]], axis=0)
        y = jnp.dot(m1, a.astype(BF16), preferred_element_type=F32)
        y_ref[0, j2 * n1:(j2 + 1) * n1, :] = y[:n1]
        y_ref[1, j2 * n1:(j2 + 1) * n1, :] = y[n1:]

    c2 = c2_ref[...].astype(BF16)
    s2 = s2_ref[...].astype(BF16)
    for k1 in range(n1):
        yr = y_ref[0, pl.ds(k1, n2, stride=n1), :]
        yi = y_ref[1, pl.ds(k1, n2, stride=n1), :]
        twc = twc_ref[:, k1:k1 + 1]
        tws = tws_ref[:, k1:k1 + 1]
        zr = (yr * twc + yi * tws).astype(BF16)
        zi = (yi * twc - yr * tws).astype(BF16)
        x = (jnp.dot(c2, zr, preferred_element_type=F32)
             + jnp.dot(s2, zi, preferred_element_type=F32))
        o_ref[pl.ds(k1, n2, stride=n1), :] = x.astype(o_ref.dtype)


def _pos_dft(u, m1, twc, tws, c2, s2, *, n1, n2):
    slabs, _, m, _ = u.shape
    return pl.pallas_call(
        functools.partial(_pos_dft_kernel, n1=n1, n2=n2),
        out_shape=jax.ShapeDtypeStruct((slabs, m, LANE), F32),
        grid=(slabs,),
        in_specs=[pl.BlockSpec((None, 2, m, LANE), lambda j: (j, 0, 0, 0)),
                  pl.BlockSpec((2 * n1, 2 * n1), lambda j: (0, 0)),
                  pl.BlockSpec((n2, n1), lambda j: (0, 0)),
                  pl.BlockSpec((n2, n1), lambda j: (0, 0)),
                  pl.BlockSpec((n2, n2), lambda j: (0, 0)),
                  pl.BlockSpec((n2, n2), lambda j: (0, 0))],
        out_specs=pl.BlockSpec((None, m, LANE), lambda j: (j, 0, 0)),
        scratch_shapes=[pltpu.VMEM((2, m, LANE), F32)],
        compiler_params=_params("parallel"),
        name="pos_dft",
    )(u, m1, twc, tws, c2, s2)


def _rope_mix(blk, gcs):
    t = blk * gcs
    return t + pltpu.roll(t, QK_ROPE, 1)


def _first_half_mask():
    lane = lax.broadcasted_iota(jnp.int32, (1, 2 * QK_ROPE), 1)
    return jnp.where(lane < QK_ROPE, 1.0, 0.0).astype(F32)


def _latent_norm(x_ref, g_ref):
    x = x_ref[...].astype(F32)
    ms = jnp.mean(x * x, axis=-1, keepdims=True)
    return (x * lax.rsqrt(ms + EPS) * g_ref[...]).astype(BF16)


def _q_prep_kernel(cq_ref, gqa_ref, w_ref, gn_ref, gr_ref, cs_ref, o_ref, *, n_valid_tiles):
    @pl.when(pl.program_id(0) >= n_valid_tiles)
    def _():
        o_ref[...] = jnp.zeros_like(o_ref)

    pl.when(pl.program_id(0) < n_valid_tiles)(
        functools.partial(_q_prep_tile, cq_ref, gqa_ref, w_ref, gn_ref, gr_ref, cs_ref, o_ref))


def _q_prep_tile(cq_ref, gqa_ref, w_ref, gn_ref, gr_ref, cs_ref, o_ref):
    xn = _latent_norm(cq_ref, gqa_ref)
    gcs = gr_ref[...] * cs_ref[...]
    gn = gn_ref[...]
    half = _first_half_mask()
    for h in range(N_HEADS):
        y = jnp.dot(xn, w_ref[:, h * HEAD_COLS:(h + 1) * HEAD_COLS], preferred_element_type=F32)
        y0 = y[:, :QK_NOPE]
        y1 = y[:, QK_NOPE:]
        ss = jnp.sum(y0 * y0 + y1 * y1 * half, axis=-1, keepdims=True)
        scale = lax.rsqrt(ss * (1.0 / QK_HEAD) + EPS) * SCORE_SCALE
        r = _rope_mix(y1, gcs)
        o_ref[h, :, :QK_NOPE] = (y0 * gn * scale).astype(o_ref.dtype)
        o_ref[h, :, QK_NOPE:] = (r[:, :QK_ROPE] * scale).astype(o_ref.dtype)


def _q_prep(p, g_qa, w_q, gn, gr, cs, *, tm, m_pad):
    m = p.shape[0]
    nt = m // tm
    row = lambda i: jnp.minimum(i, nt - 1)
    return pl.pallas_call(
        functools.partial(_q_prep_kernel, n_valid_tiles=nt),
        out_shape=jax.ShapeDtypeStruct((N_HEADS, m_pad, QK_HEAD), BF16),
        grid=(pl.cdiv(m_pad, tm),),
        in_specs=[
            pl.BlockSpec((tm, Q_LORA), lambda i: (row(i), P_Q // Q_LORA)),
            pl.BlockSpec((1, Q_LORA), lambda i: (0, 0)),
            pl.BlockSpec((Q_LORA, N_HEADS * HEAD_COLS), lambda i: (0, 0)),
            pl.BlockSpec((1, QK_NOPE), lambda i: (0, 0)),
            pl.BlockSpec((1, 2 * QK_ROPE), lambda i: (0, 0)),
            pl.BlockSpec((tm, 2 * QK_ROPE), lambda i: (row(i), 0)),
        ],
        out_specs=pl.BlockSpec((N_HEADS, tm, QK_HEAD), lambda i: (0, i, 0)),
        compiler_params=_params("parallel"),
        name="q_prep",
    )(p, g_qa, w_q, gn, gr, cs)


def _kv_prep_kernel(ckv_ref, kr_ref, gkva_ref, w_ref, gn_ref, gr_ref, cs_ref, k_ref, v_ref):
    xn = _latent_norm(ckv_ref, gkva_ref)
    kr = kr_ref[...].astype(F32)
    rope = _rope_mix(kr, gr_ref[...] * cs_ref[...])[:, :QK_ROPE]
    ss_rope = jnp.sum(kr * kr * _first_half_mask(), axis=-1, keepdims=True)
    gn = gn_ref[...]
    for h in range(N_HEADS):
        y = jnp.dot(xn, w_ref[:, h * HEAD_COLS:(h + 1) * HEAD_COLS], preferred_element_type=F32)
        kn = y[:, :QK_NOPE]
        ss = jnp.sum(kn * kn, axis=-1, keepdims=True) + ss_rope
        scale = lax.rsqrt(ss * (1.0 / QK_HEAD) + EPS)
        k_ref[h, :, :QK_NOPE] = (kn * gn * scale).astype(k_ref.dtype)
        k_ref[h, :, QK_NOPE:] = (rope * scale).astype(k_ref.dtype)
        v_ref[h] = y[:, QK_NOPE:].astype(v_ref.dtype)


def _kv_prep(p, g_kva, w_kv, gn, gr, cs, *, tm):
    m = p.shape[0]
    return pl.pallas_call(
        _kv_prep_kernel,
        out_shape=(jax.ShapeDtypeStruct((N_HEADS, m, QK_HEAD), BF16),
                   jax.ShapeDtypeStruct((N_HEADS, m, V_HEAD), BF16)),
        grid=(m // tm,),
        in_specs=[
            pl.BlockSpec((tm, KV_LORA), lambda i: (i, P_KV // KV_LORA)),
            pl.BlockSpec((tm, 2 * QK_ROPE), lambda i: (i, P_KR // (2 * QK_ROPE))),
            pl.BlockSpec((1, KV_LORA), lambda i: (0, 0)),
            pl.BlockSpec((KV_LORA, N_HEADS * HEAD_COLS), lambda i: (0, 0)),
            pl.BlockSpec((1, QK_NOPE), lambda i: (0, 0)),
            pl.BlockSpec((1, 2 * QK_ROPE), lambda i: (0, 0)),
            pl.BlockSpec((tm, 2 * QK_ROPE), lambda i: (i, 0)),
        ],
        out_specs=(pl.BlockSpec((N_HEADS, tm, QK_HEAD), lambda i: (0, i, 0)),
                   pl.BlockSpec((N_HEADS, tm, V_HEAD), lambda i: (0, i, 0))),
        compiler_params=_params("parallel"),
        name="kv_prep",
    )(p, p, g_kva, w_kv, gn, gr, cs)


def _attn_kernel(q_ref, k_ref, v_ref, o_ref, *, kc):
    q = q_ref[...]
    n = k_ref.shape[0]
    bounds = [(lo, min(lo + kc, n)) for lo in range(0, n, kc)]
    nc = len(bounds)

    def scores(c):
        lo, hi = bounds[c]
        return lax.dot_general(k_ref[lo:hi, :], q, (((1,), (1,)), ((), ())),
                               preferred_element_type=F32)

    s_next = scores(0)
    m = l = acc = None
    for c in range(nc):
        lo, hi = bounds[c]
        s = s_next
        if c + 1 < nc:
            s_next = scores(c + 1)
        m_c = jnp.max(s, axis=0, keepdims=True)
        m_new = m_c if c == 0 else jnp.maximum(m, m_c)
        e = jnp.exp2(s - m_new)
        l_c = jnp.sum(e, axis=0, keepdims=True)
        pv = lax.dot_general(v_ref[lo:hi, :], e.astype(v_ref.dtype),
                             (((0,), (0,)), ((), ())), preferred_element_type=F32)
        if c == 0:
            l, acc = l_c, pv
        else:
            alpha = jnp.exp2(m - m_new)
            l = alpha * l + l_c
            acc = alpha * acc + pv
        m = m_new
    o_ref[...] = (acc / l).T.astype(o_ref.dtype)


def _attention(q, k, v, *, tq, kc):
    _, m, _ = k.shape
    assert q.shape[1] % tq == 0
    return pl.pallas_call(
        functools.partial(_attn_kernel, kc=kc),
        out_shape=jax.ShapeDtypeStruct((m, N_HEADS * V_HEAD), BF16),
        grid=(N_HEADS, q.shape[1] // tq),
        in_specs=[
            pl.BlockSpec((None, tq, QK_HEAD), lambda h, i: (h, i, 0)),
            pl.BlockSpec((None, m, QK_HEAD), lambda h, i: (h, 0, 0)),
            pl.BlockSpec((None, m, V_HEAD), lambda h, i: (h, 0, 0)),
        ],
        out_specs=pl.BlockSpec((tq, V_HEAD), lambda h, i: (i, h)),
        compiler_params=_params("parallel", "arbitrary"),
        name="attention",
    )(q, k, v)


def _dwconv3(u, w_ref):
    n = u.shape[0]
    row = lax.broadcasted_iota(jnp.int32, u.shape, 0)
    prev = jnp.where(row == 0, 0.0, pltpu.roll(u, 1, 0))
    nxt = jnp.where(row == n - 1, 0.0, pltpu.roll(u, n - 1, 0))
    return prev * w_ref[0:1, :] + u * w_ref[1:2, :] + nxt * w_ref[2:3, :]


def _short_conv_kernel(cb_ref, cc_ref, ch_ref, w_ref, o_ref):
    u = cc_ref[...].astype(F32) * ch_ref[...].astype(F32)
    o_ref[...] = (cb_ref[...].astype(F32) * _dwconv3(u, w_ref)).astype(o_ref.dtype)


def _short_conv(p, conv_w):
    m = p.shape[0]
    spec = lambda off: pl.BlockSpec((m, LANE), lambda j: (0, off // LANE + j))
    return pl.pallas_call(
        _short_conv_kernel,
        out_shape=jax.ShapeDtypeStruct((m, C_WIDTH), BF16),
        grid=(C_WIDTH // LANE,),
        in_specs=[spec(P_CB), spec(P_CC), spec(P_CH),
                  pl.BlockSpec((3, LANE), lambda j: (0, j))],
        out_specs=pl.BlockSpec((m, LANE), lambda j: (0, j)),
        compiler_params=_params("parallel"),
        name="short_conv",
    )(p, p, p, conv_w)


def _ffn_conv_kernel(a_ref, b_ref, wa_ref, wb_ref, o_ref):
    a = _dwconv3(a_ref[...].astype(F32), wa_ref)
    b = _dwconv3(b_ref[...].astype(F32), wb_ref)
    o_ref[...] = (a * jax.nn.sigmoid(a) * b).astype(o_ref.dtype)


def _ffn_conv(up, conv_w):
    m = up.shape[0]
    nb = D_FF // LANE
    return pl.pallas_call(
        _ffn_conv_kernel,
        out_shape=jax.ShapeDtypeStruct((m, D_FF), BF16),
        grid=(nb,),
        in_specs=[pl.BlockSpec((m, LANE), lambda j: (0, j)),
                  pl.BlockSpec((m, LANE), lambda j: (0, nb + j)),
                  pl.BlockSpec((3, LANE), lambda j: (0, j)),
                  pl.BlockSpec((3, LANE), lambda j: (0, nb + j))],
        out_specs=pl.BlockSpec((m, LANE), lambda j: (0, j)),
        compiler_params=_params("parallel"),
        name="ffn_conv",
    )(up, up, conv_w, conv_w)


def _merge_kernel(fa_ref, ob_ref, yc_ref, wa_ref, wb_ref, wc_ref, g0_ref, g1_ref, g2_ref, o_ref):
    def branch(x, w_ref, g_ref):
        y = jnp.dot(x, w_ref[...], preferred_element_type=F32)
        return jax.nn.sigmoid(g_ref[...].astype(F32)) * y

    fa = jnp.concatenate([fa_ref[s] for s in range(fa_ref.shape[0])], axis=-1).astype(BF16)
    merged = (branch(fa, wa_ref, g0_ref) + branch(ob_ref[...], wb_ref, g1_ref)
              + branch(yc_ref[...], wc_ref, g2_ref))
    o_ref[...] = merged.astype(o_ref.dtype)


def _merge(fa, ob, yc, w_pa, w_pb, w_pc, p, *, tm, tn):
    m = ob.shape[0]
    nj = D_MODEL // tn
    gate = lambda b: pl.BlockSpec((tm, tn), lambda i, j: (i, b * nj + j))
    return pl.pallas_call(
        _merge_kernel,
        out_shape=jax.ShapeDtypeStruct((m, D_MODEL), BF16),
        grid=(m // tm, nj),
        in_specs=[
            pl.BlockSpec((F_WIDTH // LANE, tm, LANE), lambda i, j: (0, i, 0)),
            pl.BlockSpec((tm, N_HEADS * V_HEAD), lambda i, j: (i, 0)),
            pl.BlockSpec((tm, C_WIDTH), lambda i, j: (i, 0)),
            pl.BlockSpec((F_WIDTH, tn), lambda i, j: (0, j)),
            pl.BlockSpec((N_HEADS * V_HEAD, tn), lambda i, j: (0, j)),
            pl.BlockSpec((C_WIDTH, tn), lambda i, j: (0, j)),
            gate(0), gate(1), gate(2),
        ],
        out_specs=pl.BlockSpec((tm, tn), lambda i, j: (i, j)),
        compiler_params=_params("parallel", "arbitrary"),
        name="merge",
    )(fa, ob, yc, w_pa, w_pb, w_pc, p, p, p)


def _matmul_residual_kernel(x_ref, w_ref, h_ref, o_ref):
    o_ref[...] = h_ref[...] + jnp.dot(x_ref[...], w_ref[...], preferred_element_type=F32)


def _matmul_residual(x, w, h, *, tm, tn):
    m, k = x.shape
    n = w.shape[1]
    return pl.pallas_call(
        _matmul_residual_kernel,
        out_shape=jax.ShapeDtypeStruct((m, n), F32),
        grid=(m // tm, n // tn),
        in_specs=[
            pl.BlockSpec((tm, k), lambda i, j: (i, 0)),
            pl.BlockSpec((k, tn), lambda i, j: (0, j)),
            pl.BlockSpec((tm, tn), lambda i, j: (i, j)),
        ],
        out_specs=pl.BlockSpec((tm, tn), lambda i, j: (i, j)),
        compiler_params=_params("parallel", "arbitrary"),
        name="matmul_residual",
    )(x, w, h)


def _rotate_half_cols(w):
    half = QK_ROPE // 2
    return jnp.concatenate([-w[..., half:], w[..., :half]], axis=-1)


def _swap_halves(g):
    half = QK_ROPE // 2
    return jnp.concatenate([g[..., half:], g[..., :half]], axis=-1)


def _rope_table(n):
    inv = 1.0 / (ROPE_THETA ** (jnp.arange(0, QK_ROPE, 2, dtype=F32) / QK_ROPE))
    ang = jnp.arange(n, dtype=F32)[:, None] * inv[None, :]
    cos, sin = jnp.cos(ang), jnp.sin(ang)
    return jnp.concatenate([cos, cos, sin, sin], axis=-1)


def _dft_cos_sin(rows, cols, period):
    ang = 2.0 * np.pi * ((rows[:, None] * cols[None, :]) % period) / period
    return np.cos(ang), np.sin(ang)


def _chan_dft_table():
    idx = np.arange(F_GROUP)
    c, s = _dft_cos_sin(idx, idx, F_GROUP)
    return jnp.asarray(np.concatenate([c, -s], axis=1) / math.sqrt(F_GROUP), dtype=F32)


def _pos_dft_tables(n, n1, n2):
    i1, i2 = np.arange(n1), np.arange(n2)
    c1, s1 = _dft_cos_sin(i1, i1, n1)
    m1 = np.block([[c1, s1], [-s1, c1]])
    twc, tws = _dft_cos_sin(i2, i1, n)
    c2, s2 = _dft_cos_sin(i2, i2, n2)
    scale = 1.0 / math.sqrt(n)
    return tuple(jnp.asarray(t, dtype=F32) for t in (m1, twc, tws, c2 * scale, s2 * scale))


def _in_proj_weight(w):
    kr = w[:, OFF_KR:OFF_C]
    return jnp.concatenate([
        w[:, OFF_G:], w[:, :OFF_Q], w[:, OFF_C:OFF_G], w[:, OFF_KV:OFF_KR], w[:, OFF_Q:OFF_KV],
        kr, _rotate_half_cols(kr)], axis=1).astype(BF16)


def _q_weight(w):
    w = w.reshape(Q_LORA, N_HEADS, QK_HEAD)
    rope = w[..., QK_NOPE:]
    w = jnp.concatenate([w[..., :QK_NOPE], rope, _rotate_half_cols(rope)], axis=-1)
    return w.reshape(Q_LORA, N_HEADS * HEAD_COLS).astype(BF16)


def _qk_gains(g):
    rope = g[QK_NOPE:]
    return g[None, :QK_NOPE], jnp.concatenate([rope, _swap_halves(rope)])[None, :]


def kernel(x, meta_tokens, g_mix, w_in, g_qa, g_kva, w_uq, w_ukv, g_q, g_k, conv_c,
           w_pa, w_pb, w_pc, w_o, g_ffn, w_up, conv_ffn, w_down):
    assert x.shape[0] == 1 and x.shape[2] == D_MODEL
    depth = w_in.shape[0]
    h = jnp.concatenate([meta_tokens.astype(x.dtype), x[0]], axis=0)
    n = h.shape[0]

    tm = _row_tile(n, 912)
    tp = _row_tile(n, 144)
    tq = 4 * LANE
    kc = _row_tile(n, 1368, 8)
    tn_in = _row_tile(P_N, 1664, LANE)
    tn_up = _row_tile(2 * D_FF, 1024, LANE)

    cs = _rope_table(n)
    w_cs = _chan_dft_table()
    n1 = _row_tile(n, 64)
    n2 = n // n1
    m1, twc, tws, c2, s2 = _pos_dft_tables(n, n1, n2)

    for l in range(depth):
        p = _rms_matmul(h, g_mix[l][None], _in_proj_weight(w_in[l]), tm=tm, tn=tn_in)

        fa = _pos_dft(_chan_dft(p, w_cs, tm=tm), m1, twc, tws, c2, s2, n1=n1, n2=n2)

        gqn, gqr = _qk_gains(g_q[l])
        gkn, gkr = _qk_gains(g_k[l])
        q = _q_prep(p, g_qa[l][None], _q_weight(w_uq[l]), gqn, gqr, cs, tm=tp, m_pad=-(-n // tq) * tq)
        k, v = _kv_prep(p, g_kva[l][None], w_ukv[l].astype(BF16), gkn, gkr, cs, tm=tp)
        ob = _attention(q, k, v, tq=tq, kc=kc)

        yc = _short_conv(p, conv_c[l])

        merged = _merge(fa, ob, yc, w_pa[l].astype(BF16), w_pb[l].astype(BF16), w_pc[l].astype(BF16),
                        p, tm=tm, tn=512)
        h = _matmul_residual(merged, w_o[l].astype(BF16), h, tm=tm, tn=512)

        up = _rms_matmul(h, g_ffn[l][None], w_up[l].astype(BF16), tm=tm, tn=tn_up)
        act = _ffn_conv(up, conv_ffn[l])
        h = _matmul_residual(act, w_down[l].astype(BF16), h, tm=tm, tn=512)

    return h[None, N_META:]
```

```python
import functools
import math

import jax
import jax.numpy as jnp
import numpy as np
from jax import lax
from jax.experimental import pallas as pl
from jax.experimental.pallas import tpu as pltpu

D_MODEL = 2048
N_META = 16
F_WIDTH = D_MODEL // 2
F_GROUPS = 4
F_GROUP = F_WIDTH // F_GROUPS
N_HEADS = 16
QK_NOPE = 128
QK_ROPE = 64
QK_HEAD = QK_NOPE + QK_ROPE
V_HEAD = 128
Q_LORA = 768
KV_LORA = 512
ROPE_THETA = 10000.0
C_WIDTH = D_MODEL // 2
N_BRANCH = 3
D_FF = 5632
EPS = 1e-6

OFF_Q = F_WIDTH
OFF_KV = OFF_Q + Q_LORA
OFF_KR = OFF_KV + KV_LORA
OFF_C = OFF_KR + QK_ROPE
OFF_G = OFF_C + 3 * C_WIDTH

LANE = 128
HEAD_COLS = 2 * LANE
P_G = 0
P_A = P_G + N_BRANCH * D_MODEL
P_CB = P_A + F_WIDTH
P_CC = P_CB + C_WIDTH
P_CH = P_CC + C_WIDTH
P_KV = P_CH + C_WIDTH
P_Q = P_KV + KV_LORA
P_KR = P_Q + Q_LORA
P_N = P_KR + 2 * QK_ROPE

SCORE_SCALE = math.log2(math.e) / math.sqrt(QK_HEAD)

VMEM_LIMIT_BYTES = 56 * 1024 * 1024

BF16 = jnp.bfloat16
F32 = jnp.float32


def _params(*semantics):
    return pltpu.CompilerParams(dimension_semantics=semantics, vmem_limit_bytes=VMEM_LIMIT_BYTES)


def _row_tile(n, target, mult=16):
    best = None
    for d in range(mult, min(n, target) + 1, mult):
        if n % d == 0:
            best = d
    assert best is not None, (n, target, mult)
    return best


def _rms_matmul_kernel(x_ref, g_ref, w_ref, o_ref, xn_ref):
    @pl.when(pl.program_id(1) == 0)
    def _():
        x = x_ref[...].astype(F32)
        ms = jnp.mean(x * x, axis=-1, keepdims=True)
        xn_ref[...] = (x * lax.rsqrt(ms + EPS) * g_ref[...]).astype(xn_ref.dtype)

    o_ref[...] = jnp.dot(xn_ref[...], w_ref[...], preferred_element_type=F32).astype(o_ref.dtype)


def _rms_matmul(x, g, w, layer, *, tm, tn, out_dtype=BF16):
    m, k = x.shape
    n = w.shape[2]
    return pl.pallas_call(
        _rms_matmul_kernel,
        out_shape=jax.ShapeDtypeStruct((m, n), out_dtype),
        grid=(m // tm, n // tn),
        in_specs=[
            pl.BlockSpec((tm, k), lambda i, j: (i, 0)),
            pl.BlockSpec((1, k), lambda i, j: (0, 0)),
            pl.BlockSpec((None, k, tn), lambda i, j: (layer, 0, j)),
        ],
        out_specs=pl.BlockSpec((tm, tn), lambda i, j: (i, j)),
        scratch_shapes=[pltpu.VMEM((tm, k), BF16)],
        compiler_params=_params("parallel", "arbitrary"),
        name="rms_matmul",
    )(x, g, w)


SLABS_PER_GROUP = F_GROUP // LANE


def _chan_dft_kernel(a_ref, w_ref, o_ref):
    y = jnp.dot(a_ref[...], w_ref[...].astype(BF16), preferred_element_type=F32)
    for s in range(SLABS_PER_GROUP):
        o_ref[s, 0] = y[:, s * LANE:(s + 1) * LANE]
        o_ref[s, 1] = y[:, F_GROUP + s * LANE:F_GROUP + (s + 1) * LANE]


def _chan_dft(p, w_cs, *, tm):
    m = p.shape[0]
    base = P_A // F_GROUP
    return pl.pallas_call(
        _chan_dft_kernel,
        out_shape=jax.ShapeDtypeStruct((F_WIDTH // LANE, 2, m, LANE), F32),
        grid=(m // tm, F_GROUPS),
        in_specs=[
            pl.BlockSpec((tm, F_GROUP), lambda i, g: (i, base + g)),
            pl.BlockSpec((F_GROUP, 2 * F_GROUP), lambda i, g: (0, 0)),
        ],
        out_specs=pl.BlockSpec((SLABS_PER_GROUP, 2, tm, LANE), lambda i, g: (g, 0, i, 0)),
        compiler_params=_params("parallel", "arbitrary"),
        name="chan_dft",
    )(p, w_cs)


def _pos_dft_kernel(u_ref, m1_ref, twc_ref, tws_ref, c2_ref, s2_ref, o_ref, y_ref, *, n1, n2):
    m1 = m1_ref[...].astype(BF16)
    for j2 in range(n2):
        a = jnp.concatenate([u_ref[0, pl.ds(j2, n1, stride=n2), :],
                             u_ref[1, pl.ds(j2, n1, stride=n2), :]], axis=0)
        y = jnp.dot(m1, a.astype(BF16), preferred_element_type=F32)
        y_ref[0, j2 * n1:(j2 + 1) * n1, :] = y[:n1]
        y_ref[1, j2 * n1:(j2 + 1) * n1, :] = y[n1:]

    c2 = c2_ref[...].astype(BF16)
    s2 = s2_ref[...].astype(BF16)
    for k1 in range(n1):
        yr = y_ref[0, pl.ds(k1, n2, stride=n1), :]
        yi = y_ref[1, pl.ds(k1, n2, stride=n1), :]
        twc = twc_ref[:, k1:k1 + 1]
        tws = tws_ref[:, k1:k1 + 1]
        zr = (yr * twc + yi * tws).astype(BF16)
        zi = (yi * twc - yr * tws).astype(BF16)
        x = (jnp.dot(c2, zr, preferred_element_type=F32)
             + jnp.dot(s2, zi, preferred_element_type=F32))
        o_ref[pl.ds(k1, n2, stride=n1), :] = x.astype(o_ref.dtype)


def _pos_dft(u, m1, twc, tws, c2, s2, *, n1, n2):
    slabs, _, m, _ = u.shape
    return pl.pallas_call(
        functools.partial(_pos_dft_kernel, n1=n1, n2=n2),
        out_shape=jax.ShapeDtypeStruct((slabs, m, LANE), F32),
        grid=(slabs,),
        in_specs=[pl.BlockSpec((None, 2, m, LANE), lambda j: (j, 0, 0, 0)),
                  pl.BlockSpec((2 * n1, 2 * n1), lambda j: (0, 0)),
                  pl.BlockSpec((n2, n1), lambda j: (0, 0)),
                  pl.BlockSpec((n2, n1), lambda j: (0, 0)),
                  pl.BlockSpec((n2, n2), lambda j: (0, 0)),
                  pl.BlockSpec((n2, n2), lambda j: (0, 0))],
        out_specs=pl.BlockSpec((None, m, LANE), lambda j: (j, 0, 0)),
        scratch_shapes=[pltpu.VMEM((2, m, LANE), F32)],
        compiler_params=_params("parallel"),
        name="pos_dft",
    )(u, m1, twc, tws, c2, s2)


def _rope_mix(blk, gcs):
    t = blk * gcs
    return t + pltpu.roll(t, QK_ROPE, 1)


def _first_half_mask():
    lane = lax.broadcasted_iota(jnp.int32, (1, 2 * QK_ROPE), 1)
    return jnp.where(lane < QK_ROPE, 1.0, 0.0).astype(F32)


def _latent_norm(x_ref, g_ref):
    x = x_ref[...].astype(F32)
    ms = jnp.mean(x * x, axis=-1, keepdims=True)
    return (x * lax.rsqrt(ms + EPS) * g_ref[...]).astype(BF16)


def _q_prep_kernel(cq_ref, gqa_ref, w_ref, gn_ref, gr_ref, cs_ref, o_ref, *, n_valid_tiles):
    @pl.when(pl.program_id(0) >= n_valid_tiles)
    def _():
        o_ref[...] = jnp.zeros_like(o_ref)

    pl.when(pl.program_id(0) < n_valid_tiles)(
        functools.partial(_q_prep_tile, cq_ref, gqa_ref, w_ref, gn_ref, gr_ref, cs_ref, o_ref))


def _q_prep_tile(cq_ref, gqa_ref, w_ref, gn_ref, gr_ref, cs_ref, o_ref):
    xn = _latent_norm(cq_ref, gqa_ref)
    gcs = gr_ref[...] * cs_ref[...]
    gn = gn_ref[...]
    half = _first_half_mask()
    for h in range(N_HEADS):
        y = jnp.dot(xn, w_ref[:, h * HEAD_COLS:(h + 1) * HEAD_COLS], preferred_element_type=F32)
        y0 = y[:, :QK_NOPE]
        y1 = y[:, QK_NOPE:]
        ss = jnp.sum(y0 * y0 + y1 * y1 * half, axis=-1, keepdims=True)
        scale = lax.rsqrt(ss * (1.0 / QK_HEAD) + EPS) * SCORE_SCALE
        r = _rope_mix(y1, gcs)
        o_ref[h, :, :QK_NOPE] = (y0 * gn * scale).astype(o_ref.dtype)
        o_ref[h, :, QK_NOPE:] = (r[:, :QK_ROPE] * scale).astype(o_ref.dtype)


def _q_prep(p, g_qa, w_q, layer, gn, gr, cs, *, tm, m_pad):
    m = p.shape[0]
    nt = m // tm
    row = lambda i: jnp.minimum(i, nt - 1)
    return pl.pallas_call(
        functools.partial(_q_prep_kernel, n_valid_tiles=nt),
        out_shape=jax.ShapeDtypeStruct((N_HEADS, m_pad, QK_HEAD), BF16),
        grid=(pl.cdiv(m_pad, tm),),
        in_specs=[
            pl.BlockSpec((tm, Q_LORA), lambda i: (row(i), P_Q // Q_LORA)),
            pl.BlockSpec((1, Q_LORA), lambda i: (0, 0)),
            pl.BlockSpec((None, Q_LORA, N_HEADS * HEAD_COLS), lambda i: (layer, 0, 0)),
            pl.BlockSpec((1, QK_NOPE), lambda i: (0, 0)),
            pl.BlockSpec((1, 2 * QK_ROPE), lambda i: (0, 0)),
            pl.BlockSpec((tm, 2 * QK_ROPE), lambda i: (row(i), 0)),
        ],
        out_specs=pl.BlockSpec((N_HEADS, tm, QK_HEAD), lambda i: (0, i, 0)),
        compiler_params=_params("parallel"),
        name="q_prep",
    )(p, g_qa, w_q, gn, gr, cs)


def _kv_prep_kernel(ckv_ref, kr_ref, gkva_ref, w_ref, gn_ref, gr_ref, cs_ref, k_ref, v_ref):
    xn = _latent_norm(ckv_ref, gkva_ref)
    kr = kr_ref[...].astype(F32)
    rope = _rope_mix(kr, gr_ref[...] * cs_ref[...])[:, :QK_ROPE]
    ss_rope = jnp.sum(kr * kr * _first_half_mask(), axis=-1, keepdims=True)
    gn = gn_ref[...]
    for h in range(N_HEADS):
        y = jnp.dot(xn, w_ref[:, h * HEAD_COLS:(h + 1) * HEAD_COLS], preferred_element_type=F32)
        kn = y[:, :QK_NOPE]
        ss = jnp.sum(kn * kn, axis=-1, keepdims=True) + ss_rope
        scale = lax.rsqrt(ss * (1.0 / QK_HEAD) + EPS)
        k_ref[h, :, :QK_NOPE] = (kn * gn * scale).astype(k_ref.dtype)
        k_ref[h, :, QK_NOPE:] = (rope * scale).astype(k_ref.dtype)
        v_ref[h] = y[:, QK_NOPE:].astype(v_ref.dtype)


def _kv_prep(p, g_kva, w_kv, layer, gn, gr, cs, *, tm):
    m = p.shape[0]
    return pl.pallas_call(
        _kv_prep_kernel,
        out_shape=(jax.ShapeDtypeStruct((N_HEADS, m, QK_HEAD), BF16),
                   jax.ShapeDtypeStruct((N_HEADS, m, V_HEAD), BF16)),
        grid=(m // tm,),
        in_specs=[
            pl.BlockSpec((tm, KV_LORA), lambda i: (i, P_KV // KV_LORA)),
            pl.BlockSpec((tm, 2 * QK_ROPE), lambda i: (i, P_KR // (2 * QK_ROPE))),
            pl.BlockSpec((1, KV_LORA), lambda i: (0, 0)),
            pl.BlockSpec((None, KV_LORA, N_HEADS * HEAD_COLS), lambda i: (layer, 0, 0)),
            pl.BlockSpec((1, QK_NOPE), lambda i: (0, 0)),
            pl.BlockSpec((1, 2 * QK_ROPE), lambda i: (0, 0)),
            pl.BlockSpec((tm, 2 * QK_ROPE), lambda i: (i, 0)),
        ],
        out_specs=(pl.BlockSpec((N_HEADS, tm, QK_HEAD), lambda i: (0, i, 0)),
                   pl.BlockSpec((N_HEADS, tm, V_HEAD), lambda i: (0, i, 0))),
        compiler_params=_params("parallel"),
        name="kv_prep",
    )(p, p, g_kva, w_kv, gn, gr, cs)


def _attn_kernel(q_ref, k_ref, v_ref, o_ref, *, kc):
    q = q_ref[...]
    n = k_ref.shape[0]
    bounds = [(lo, min(lo + kc, n)) for lo in range(0, n, kc)]
    nc = len(bounds)

    def scores(c):
        lo, hi = bounds[c]
        return lax.dot_general(k_ref[lo:hi, :], q, (((1,), (1,)), ((), ())),
                               preferred_element_type=F32)

    s_next = scores(0)
    m = l = acc = None
    for c in range(nc):
        lo, hi = bounds[c]
        s = s_next
        if c + 1 < nc:
            s_next = scores(c + 1)
        m_c = jnp.max(s, axis=0, keepdims=True)
        m_new = m_c if c == 0 else jnp.maximum(m, m_c)
        e = jnp.exp2(s - m_new)
        l_c = jnp.sum(e, axis=0, keepdims=True)
        pv = lax.dot_general(v_ref[lo:hi, :], e.astype(v_ref.dtype),
                             (((0,), (0,)), ((), ())), preferred_element_type=F32)
        if c == 0:
            l, acc = l_c, pv
        else:
            alpha = jnp.exp2(m - m_new)
            l = alpha * l + l_c
            acc = alpha * acc + pv
        m = m_new
    o_ref[...] = (acc / l).T.astype(o_ref.dtype)


def _attention(q, k, v, *, tq, kc):
    _, m, _ = k.shape
    assert q.shape[1] % tq == 0
    return pl.pallas_call(
        functools.partial(_attn_kernel, kc=kc),
        out_shape=jax.ShapeDtypeStruct((m, N_HEADS * V_HEAD), BF16),
        grid=(N_HEADS, q.shape[1] // tq),
        in_specs=[
            pl.BlockSpec((None, tq, QK_HEAD), lambda h, i: (h, i, 0)),
            pl.BlockSpec((None, m, QK_HEAD), lambda h, i: (h, 0, 0)),
            pl.BlockSpec((None, m, V_HEAD), lambda h, i: (h, 0, 0)),
        ],
        out_specs=pl.BlockSpec((tq, V_HEAD), lambda h, i: (i, h)),
        compiler_params=_params("parallel", "arbitrary"),
        name="attention",
    )(q, k, v)


def _dwconv3(u, w_ref):
    n = u.shape[0]
    row = lax.broadcasted_iota(jnp.int32, u.shape, 0)
    prev = jnp.where(row == 0, 0.0, pltpu.roll(u, 1, 0))
    nxt = jnp.where(row == n - 1, 0.0, pltpu.roll(u, n - 1, 0))
    return prev * w_ref[0:1, :] + u * w_ref[1:2, :] + nxt * w_ref[2:3, :]


def _short_conv_kernel(cb_ref, cc_ref, ch_ref, w_ref, o_ref):
    u = cc_ref[...].astype(F32) * ch_ref[...].astype(F32)
    o_ref[...] = (cb_ref[...].astype(F32) * _dwconv3(u, w_ref)).astype(o_ref.dtype)


def _short_conv(p, conv_w):
    m = p.shape[0]
    spec = lambda off: pl.BlockSpec((m, LANE), lambda j: (0, off // LANE + j))
    return pl.pallas_call(
        _short_conv_kernel,
        out_shape=jax.ShapeDtypeStruct((m, C_WIDTH), BF16),
        grid=(C_WIDTH // LANE,),
        in_specs=[spec(P_CB), spec(P_CC), spec(P_CH),
                  pl.BlockSpec((3, LANE), lambda j: (0, j))],
        out_specs=pl.BlockSpec((m, LANE), lambda j: (0, j)),
        compiler_params=_params("parallel"),
        name="short_conv",
    )(p, p, p, conv_w)


def _ffn_conv_kernel(a_ref, b_ref, wa_ref, wb_ref, o_ref):
    a = _dwconv3(a_ref[...].astype(F32), wa_ref)
    b = _dwconv3(b_ref[...].astype(F32), wb_ref)
    o_ref[...] = (a * jax.nn.sigmoid(a) * b).astype(o_ref.dtype)


def _ffn_conv(up, conv_w):
    m = up.shape[0]
    nb = D_FF // LANE
    return pl.pallas_call(
        _ffn_conv_kernel,
        out_shape=jax.ShapeDtypeStruct((m, D_FF), BF16),
        grid=(nb,),
        in_specs=[pl.BlockSpec((m, LANE), lambda j: (0, j)),
                  pl.BlockSpec((m, LANE), lambda j: (0, nb + j)),
                  pl.BlockSpec((3, LANE), lambda j: (0, j)),
                  pl.BlockSpec((3, LANE), lambda j: (0, nb + j))],
        out_specs=pl.BlockSpec((m, LANE), lambda j: (0, j)),
        compiler_params=_params("parallel"),
        name="ffn_conv",
    )(up, up, conv_w, conv_w)


def _merge_kernel(fa_ref, ob_ref, yc_ref, wa_ref, wb_ref, wc_ref, g0_ref, g1_ref, g2_ref, o_ref):
    def branch(x, w_ref, g_ref):
        y = jnp.dot(x, w_ref[...], preferred_element_type=F32)
        return jax.nn.sigmoid(g_ref[...].astype(F32)) * y

    fa = jnp.concatenate([fa_ref[s] for s in range(fa_ref.shape[0])], axis=-1).astype(BF16)
    merged = (branch(fa, wa_ref, g0_ref) + branch(ob_ref[...], wb_ref, g1_ref)
              + branch(yc_ref[...], wc_ref, g2_ref))
    o_ref[...] = merged.astype(o_ref.dtype)


def _merge(fa, ob, yc, w_pa, w_pb, w_pc, layer, p, *, tm, tn):
    m = ob.shape[0]
    nj = D_MODEL // tn
    gate = lambda b: pl.BlockSpec((tm, tn), lambda i, j: (i, b * nj + j))
    return pl.pallas_call(
        _merge_kernel,
        out_shape=jax.ShapeDtypeStruct((m, D_MODEL), BF16),
        grid=(m // tm, nj),
        in_specs=[
            pl.BlockSpec((F_WIDTH // LANE, tm, LANE), lambda i, j: (0, i, 0)),
            pl.BlockSpec((tm, N_HEADS * V_HEAD), lambda i, j: (i, 0)),
            pl.BlockSpec((tm, C_WIDTH), lambda i, j: (i, 0)),
            pl.BlockSpec((None, F_WIDTH, tn), lambda i, j: (layer, 0, j)),
            pl.BlockSpec((None, N_HEADS * V_HEAD, tn), lambda i, j: (layer, 0, j)),
            pl.BlockSpec((None, C_WIDTH, tn), lambda i, j: (layer, 0, j)),
            gate(0), gate(1), gate(2),
        ],
        out_specs=pl.BlockSpec((tm, tn), lambda i, j: (i, j)),
        compiler_params=_params("parallel", "arbitrary"),
        name="merge",
    )(fa, ob, yc, w_pa, w_pb, w_pc, p, p, p)


def _matmul_residual_kernel(x_ref, w_ref, h_ref, o_ref):
    o_ref[...] = h_ref[...] + jnp.dot(x_ref[...], w_ref[...], preferred_element_type=F32)


def _matmul_residual(x, w, layer, h, *, tm, tn):
    m, k = x.shape
    n = w.shape[2]
    return pl.pallas_call(
        _matmul_residual_kernel,
        out_shape=jax.ShapeDtypeStruct((m, n), F32),
        grid=(m // tm, n // tn),
        in_specs=[
            pl.BlockSpec((tm, k), lambda i, j: (i, 0)),
            pl.BlockSpec((None, k, tn), lambda i, j: (layer, 0, j)),
            pl.BlockSpec((tm, tn), lambda i, j: (i, j)),
        ],
        out_specs=pl.BlockSpec((tm, tn), lambda i, j: (i, j)),
        compiler_params=_params("parallel", "arbitrary"),
        name="matmul_residual",
    )(x, w, h)


def _rotate_half_cols(w):
    half = QK_ROPE // 2
    return jnp.concatenate([-w[..., half:], w[..., :half]], axis=-1)


def _swap_halves(g):
    half = QK_ROPE // 2
    return jnp.concatenate([g[..., half:], g[..., :half]], axis=-1)


def _rope_table(n):
    inv = 1.0 / (ROPE_THETA ** (jnp.arange(0, QK_ROPE, 2, dtype=F32) / QK_ROPE))
    ang = jnp.arange(n, dtype=F32)[:, None] * inv[None, :]
    cos, sin = jnp.cos(ang), jnp.sin(ang)
    return jnp.concatenate([cos, cos, sin, sin], axis=-1)


def _dft_cos_sin(rows, cols, period):
    ang = 2.0 * np.pi * ((rows[:, None] * cols[None, :]) % period) / period
    return np.cos(ang), np.sin(ang)


def _chan_dft_table():
    idx = np.arange(F_GROUP)
    c, s = _dft_cos_sin(idx, idx, F_GROUP)
    return jnp.asarray(np.concatenate([c, -s], axis=1) / math.sqrt(F_GROUP), dtype=F32)


def _pos_dft_tables(n, n1, n2):
    i1, i2 = np.arange(n1), np.arange(n2)
    c1, s1 = _dft_cos_sin(i1, i1, n1)
    m1 = np.block([[c1, s1], [-s1, c1]])
    twc, tws = _dft_cos_sin(i2, i1, n)
    c2, s2 = _dft_cos_sin(i2, i2, n2)
    scale = 1.0 / math.sqrt(n)
    return tuple(jnp.asarray(t, dtype=F32) for t in (m1, twc, tws, c2 * scale, s2 * scale))


def _in_proj_weight(w):
    w = w.astype(BF16)
    kr = w[..., OFF_KR:OFF_C]
    return jnp.concatenate([
        w[..., OFF_G:], w[..., :OFF_Q], w[..., OFF_C:OFF_G], w[..., OFF_KV:OFF_KR], w[..., OFF_Q:OFF_KV],
        kr, _rotate_half_cols(kr)], axis=-1)


def _q_weight(w):
    depth = w.shape[0]
    w = w.astype(BF16).reshape(depth, Q_LORA, N_HEADS, QK_HEAD)
    rope = w[..., QK_NOPE:]
    w = jnp.concatenate([w[..., :QK_NOPE], rope, _rotate_half_cols(rope)], axis=-1)
    return w.reshape(depth, Q_LORA, N_HEADS * HEAD_COLS)


def _qk_gains(g):
    rope = g[QK_NOPE:]
    return g[None, :QK_NOPE], jnp.concatenate([rope, _swap_halves(rope)])[None, :]


def kernel(x, meta_tokens, g_mix, w_in, g_qa, g_kva, w_uq, w_ukv, g_q, g_k, conv_c,
           w_pa, w_pb, w_pc, w_o, g_ffn, w_up, conv_ffn, w_down):
    assert x.shape[0] == 1 and x.shape[2] == D_MODEL
    depth = w_in.shape[0]
    h = jnp.concatenate([meta_tokens.astype(x.dtype), x[0]], axis=0)
    n = h.shape[0]

    tm = _row_tile(n, 912)
    tp = _row_tile(n, 304)
    tq = 4 * LANE
    kc = _row_tile(n, 1368, 8)
    tn_in = _row_tile(P_N, 1664, LANE)
    tn_up = _row_tile(2 * D_FF, 1024, LANE)

    cs = _rope_table(n)
    w_cs = _chan_dft_table()
    n1 = _row_tile(n, 64)
    n2 = n // n1
    m1, twc, tws, c2, s2 = _pos_dft_tables(n, n1, n2)

    w_in_b = _in_proj_weight(w_in)
    w_uq_b = _q_weight(w_uq)
    w_ukv_b, w_pa_b, w_pb_b, w_pc_b, w_o_b, w_up_b, w_down_b = (
        w.astype(BF16) for w in (w_ukv, w_pa, w_pb, w_pc, w_o, w_up, w_down))

    for l in range(depth):
        p = _rms_matmul(h, g_mix[l][None], w_in_b, l, tm=tm, tn=tn_in)

        fa = _pos_dft(_chan_dft(p, w_cs, tm=tm), m1, twc, tws, c2, s2, n1=n1, n2=n2)

        gqn, gqr = _qk_gains(g_q[l])
        gkn, gkr = _qk_gains(g_k[l])
        q = _q_prep(p, g_qa[l][None], w_uq_b, l, gqn, gqr, cs, tm=tp, m_pad=-(-n // tq) * tq)
        k, v = _kv_prep(p, g_kva[l][None], w_ukv_b, l, gkn, gkr, cs, tm=tp)
        ob = _attention(q, k, v, tq=tq, kc=kc)

        yc = _short_conv(p, conv_c[l])

        merged = _merge(fa, ob, yc, w_pa_b, w_pb_b, w_pc_b, l, p, tm=tm, tn=512)
        h = _matmul_residual(merged, w_o_b, l, h, tm=tm, tn=512)

        up = _rms_matmul(h, g_ffn[l][None], w_up_b, l, tm=tm, tn=tn_up)
        act = _ffn_conv(up, conv_ffn[l])
        h = _matmul_residual(act, w_down_b, l, h, tm=tm, tn=512)

    return h[None, N_META:]
```
